```python
import jax, jax.numpy as jnp
from jax import lax
import numpy as np

D_MODEL = 1024
BATCH = 8
SEQ = 2048
DEPTH = 4
DEC_BATCH = 128
DEC_SEQ = 1
PAST_LEN = 16384
PAGE_SIZE = 128

BRANCH_W = D_MODEL // 2
N_BRANCH = 3
CONV_W = 3
CONV_GROUPS = 8
CHUNK = 128
SGU_HEADS = 8
SGU_HEAD_DIM = BRANCH_W // SGU_HEADS
POOL_WINDOWS = (2, 4, 8, 16)
POOL_GROUPS = len(POOL_WINDOWS)
POOL_GROUP_DIM = BRANCH_W // POOL_GROUPS
POOL_BUF = max(POOL_WINDOWS) - 1
EPS = 1e-6
IN_SIZES = (BRANCH_W, BRANCH_W, BRANCH_W, BRANCH_W,
            BRANCH_W, BRANCH_W, BRANCH_W,
            BRANCH_W, BRANCH_W,
            D_MODEL, D_MODEL, D_MODEL)
IN_COLS = sum(IN_SIZES)
IN_SPLITS = [int(s) for s in np.cumsum(IN_SIZES)[:-1]]

kernel_name = 'hybrid_conv_sgu_pool_decoder_step'


def _rmsnorm(x, g):
    xf = x.astype(jnp.float32)
    y = xf * lax.rsqrt(jnp.mean(xf * xf, axis=-1, keepdims=True) + EPS)
    return (y * g.astype(jnp.float32)).astype(x.dtype)


def _layernorm(x, g, b):
    xf = x.astype(jnp.float32)
    mu = jnp.mean(xf, axis=-1, keepdims=True)
    var = jnp.mean(jnp.square(xf - mu), axis=-1, keepdims=True)
    y = (xf - mu) * lax.rsqrt(var + EPS)
    return (y * g.astype(jnp.float32) + b.astype(jnp.float32)).astype(x.dtype)


def _short_conv(z, prev, w, b):
    T = z.shape[1]
    full = jnp.concatenate([prev.astype(z.dtype), z], axis=1)
    out = b + sum(full[:, k:k + T] * w[k] for k in range(CONV_W))
    return out, full[:, -(CONV_W - 1):]


def _spatial_gate(u, v, ws, bs):
    N, T, Wd = v.shape
    L = CHUNK if T >= CHUNK else T
    nc = -(-T // L)
    vp = jnp.pad(v, ((0, 0), (0, nc * L - T), (0, 0)))
    vr = vp.reshape(N, nc, L, SGU_HEADS, SGU_HEAD_DIM)
    mask = jnp.tril(jnp.ones((L, L), dtype=bool))
    ws_m = jnp.where(mask, ws[:, :L, :L], jnp.zeros((), ws.dtype))
    mixed = jnp.einsum('hts,ncshd->ncthd', ws_m, vr) + jnp.transpose(bs[:, :L])[:, :, None]
    mixed = mixed.reshape(N, nc * L, Wd)[:, :T]
    return u * mixed


def _multiscale_pool(xc, prev, start):
    N, T, Wd = xc.shape
    full = jnp.concatenate([prev.astype(xc.dtype), xc], axis=1)
    fullf = full.astype(jnp.float32)
    cs = jnp.concatenate([jnp.zeros((N, 1, Wd), jnp.float32), jnp.cumsum(fullf, axis=1)], axis=1)
    pos = start + jnp.arange(T)
    outs = []
    for g, w in enumerate(POOL_WINDOWS):
        sl = slice(g * POOL_GROUP_DIM, (g + 1) * POOL_GROUP_DIM)
        s = cs[:, POOL_BUF + 1:, sl] - cs[:, POOL_BUF + 1 - w:POOL_BUF + 1 - w + T, sl]
        cnt = jnp.minimum(w, pos + 1).astype(jnp.float32)[None, :, None]
        outs.append(s / cnt)
    pooled = jnp.concatenate(outs, axis=-1)
    mixed = (pooled - xc.astype(jnp.float32)).astype(xc.dtype)
    return mixed, full[:, -POOL_BUF:]


def _layer(x, c, start, conv_prev, pool_prev, w_ada, b_ada, norm_g, w_in, conv_w, conv_b,
           lnv_g, lnv_b, sgu_w, sgu_b, pool_w, pool_b, pool_scale, w_branch, w_out):
    N, T, _ = x.shape
    mod = jax.nn.silu(c) @ w_ada + b_ada
    shift, scale, gate = jnp.split(mod[:, None, :], 3, axis=-1)
    h = _rmsnorm(x, norm_g) * (1 + scale) + shift
    p = h @ w_in
    a_b, a_c, a_h, a_g, b_u, b_v, b_g, c_x, c_g, m_a, m_b, m_c = jnp.split(p, IN_SPLITS, axis=-1)
    conv_out, conv_new = _short_conv(a_c * a_h, conv_prev, conv_w, conv_b)
    y_a = a_b * conv_out * jax.nn.silu(a_g)
    u = jax.nn.gelu(b_u)
    v = _layernorm(jax.nn.gelu(b_v), lnv_g, lnv_b)
    y_b = _spatial_gate(u, v, sgu_w, sgu_b) * jax.nn.silu(b_g)
    v_new = v[:, ((T - 1) // CHUNK) * CHUNK:]
    pm, pool_new = _multiscale_pool(c_x, pool_prev, start)
    y_c = jnp.einsum('ntgi,gio->ntgo', pm.reshape(N, T, POOL_GROUPS, POOL_GROUP_DIM), pool_w)
    y_c = (y_c.reshape(N, T, BRANCH_W) + pool_b) * pool_scale * jax.nn.silu(c_g)
    merged = (jax.nn.sigmoid(m_a) * (y_a @ w_branch[0])
              + jax.nn.sigmoid(m_b) * (y_b @ w_branch[1])
              + jax.nn.sigmoid(m_c) * (y_c @ w_branch[2]))
    x = x + gate * (merged @ w_out)
    return x, conv_new, pool_new, v_new


def setup_inputs(seed: int = 0) -> dict:
    key = jax.random.key(seed)
    ks = jax.random.split(key, 24)
    nrm = lambda k, s, sc: jax.random.normal(k, s, jnp.float32) * sc
    D, W = D_MODEL, BRANCH_W
    return {
        'x_prompt': nrm(ks[0], (BATCH, SEQ, D), 1.0),
        'x_sample': nrm(ks[1], (DEC_BATCH, DEC_SEQ, D), 1.0),
        'c_prompt': nrm(ks[2], (BATCH, D), 1.0),
        'c_sample': nrm(ks[3], (DEC_BATCH, D), 1.0),
        'state_conv': nrm(ks[4], (DEPTH, DEC_BATCH, CONV_W - 1, W), 1.0),
        'state_pool': nrm(ks[5], (DEPTH, DEC_BATCH, POOL_BUF, W), 1.0),
        'w_ada': nrm(ks[6], (DEPTH, D, 3 * D), 0.5 * D ** -0.5),
        'b_ada': nrm(ks[7], (DEPTH, 3 * D), 0.02),
        'norm_g': 1.0 + nrm(ks[8], (DEPTH, D), 0.02),
        'w_in': nrm(ks[9], (DEPTH, D, IN_COLS), D ** -0.5),
        'conv_w': nrm(ks[10], (DEPTH, CONV_W, W), CONV_W ** -0.5),
        'conv_b': nrm(ks[11], (DEPTH, W), 0.02),
        'lnv_g': 1.0 + nrm(ks[12], (DEPTH, W), 0.02),
        'lnv_b': nrm(ks[13], (DEPTH, W), 0.02),
        'sgu_w': nrm(ks[14], (DEPTH, SGU_HEADS, CHUNK, CHUNK), CHUNK ** -0.5),
        'sgu_b': 1.0 + nrm(ks[15], (DEPTH, SGU_HEADS, CHUNK), 0.02),
        'pool_w': nrm(ks[16], (DEPTH, POOL_GROUPS, POOL_GROUP_DIM, POOL_GROUP_DIM), POOL_GROUP_DIM ** -0.5),
        'pool_b': nrm(ks[17], (DEPTH, W), 0.02),
        'pool_scale': 1.0 + nrm(ks[18], (DEPTH, W), 0.1),
        'w_branch': nrm(ks[19], (DEPTH, N_BRANCH, W, D), W ** -0.5),
        'w_out': nrm(ks[20], (DEPTH, D, D), D ** -0.5),
        'final_g': 1.0 + nrm(ks[21], (D,), 0.02),
    }


def reference(x_prompt, x_sample, c_prompt, c_sample, state_conv, state_pool, w_ada, b_ada, norm_g,
              w_in, conv_w, conv_b, lnv_g, lnv_b, sgu_w, sgu_b, pool_w, pool_b, pool_scale,
              w_branch, w_out, final_g):
    xp, xs = x_prompt, x_sample
    conv_p, conv_s, pool_p, pool_s, v_p, v_s = [], [], [], [], [], []
    zero_conv = jnp.zeros((xp.shape[0], CONV_W - 1, BRANCH_W), xp.dtype)
    zero_pool = jnp.zeros((xp.shape[0], POOL_BUF, BRANCH_W), xp.dtype)
    for l in range(DEPTH):
        params = (w_ada[l], b_ada[l], norm_g[l], w_in[l], conv_w[l], conv_b[l], lnv_g[l], lnv_b[l],
                  sgu_w[l], sgu_b[l], pool_w[l], pool_b[l], pool_scale[l], w_branch[l], w_out[l])
        xp, cp, pp, vp = _layer(xp, c_prompt, 0, zero_conv, zero_pool, *params)
        xs, cs_, ps, vs = _layer(xs, c_sample, PAST_LEN, state_conv[l], state_pool[l], *params)
        conv_p.append(cp); conv_s.append(cs_); pool_p.append(pp); pool_s.append(ps)
        v_p.append(vp); v_s.append(vs)
    y_prompt = _rmsnorm(xp, final_g)
    y_sample = _rmsnorm(xs, final_g)
    return (y_prompt, y_sample, jnp.stack(conv_p), jnp.stack(conv_s), jnp.stack(pool_p),
            jnp.stack(pool_s), jnp.stack(v_p), jnp.stack(v_s))
```

```python
import functools
import math

import jax
import jax.numpy as jnp
from jax import lax
from jax.experimental import pallas as pl
from jax.experimental.pallas import tpu as pltpu

F32 = jnp.float32
BF16 = jnp.bfloat16

CONV_W = 3
CHUNK = 128
SGU_HEADS = 8
POOL_WINDOWS = (2, 4, 8, 16)
POOL_BUF = max(POOL_WINDOWS) - 1
PAST_LEN = 16384
EPS = 1e-6

SUBLANES = 8
SEQ_TILE = 256
HEADS_PER_DOT = 4
VMEM_LIMIT_BYTES = 52 * 1024 * 1024


def _sigmoid(x):
    return 0.5 * (1.0 + jnp.tanh(0.5 * x))


def _silu(x):
    return x * _sigmoid(x)


def _gelu(x):
    c = math.sqrt(2.0 / math.pi)
    return 0.5 * x * (1.0 + jnp.tanh(c * (x + 0.044715 * (x * x * x))))


def _dot(a, b):
    return jnp.dot(a, b, preferred_element_type=F32)


def _modulated_norm(x, norm_g, shift, scale):
    ms = jnp.mean(x * x, axis=-1, keepdims=True)
    return (x * lax.rsqrt(ms + EPS)) * (norm_g * (1.0 + scale)) + shift


def _layernorm(x, g, b):
    mu = jnp.mean(x, axis=-1, keepdims=True)
    d = x - mu
    var = jnp.mean(d * d, axis=-1, keepdims=True)
    return d * lax.rsqrt(var + EPS) * g + b


def _mod_kernel(c_ref, w_ref, b_ref, o_ref):
    s = _silu(c_ref[...]).astype(BF16)
    o_ref[0] = _dot(s, w_ref[0].astype(BF16)) + b_ref[0]


def _mod_call(c_all, w_ada, b_ada):
    depth, d, d3 = w_ada.shape
    rows = c_all.shape[0]
    col_block = d3 // 2
    return pl.pallas_call(
        _mod_kernel,
        grid=(depth, d3 // col_block),
        in_specs=[
            pl.BlockSpec((rows, d), lambda l, j: (0, 0)),
            pl.BlockSpec((1, d, col_block), lambda l, j: (l, 0, j)),
            pl.BlockSpec((1, 1, col_block), lambda l, j: (l, 0, j)),
        ],
        out_specs=pl.BlockSpec((1, rows, col_block), lambda l, j: (l, 0, j)),
        out_shape=jax.ShapeDtypeStruct((depth, rows, d3), F32),
        compiler_params=pltpu.CompilerParams(
            dimension_semantics=("arbitrary", "arbitrary"),
            vmem_limit_bytes=VMEM_LIMIT_BYTES),
        name="adaln_mod",
    )(c_all, w_ada, b_ada.reshape(depth, 1, d3))


def _prompt_kernel(x_ref, mod_ref, ng_ref, win_ref, convw_ref, convb_ref, lng_ref, lnb_ref,
                   sguw_ref, sgub_ref, poolw_ref, poolb_ref, pools_ref, wbr_ref, wout_ref, fg_ref,
                   xo_ref, convo_ref, poolo_ref, vo_ref,
                   zbuf, cbuf, *, layer, final):
    t = pl.program_id(1)
    last_t = pl.num_programs(1) - 1
    tm, d = x_ref.shape[1], x_ref.shape[2]
    w = d // 2
    zoff = SUBLANES
    coff = 2 * SUBLANES

    @pl.when(t == 0)
    def _():
        zbuf[0:zoff, :] = jnp.zeros((zoff, w), F32)
        cbuf[0:coff, :] = jnp.zeros((coff, w), F32)

    x = x_ref[0]
    mod = mod_ref[0]
    shift, scale, gate = mod[:, 0:d], mod[:, d:2 * d], mod[:, 2 * d:3 * d]
    hb = _modulated_norm(x, ng_ref[layer], shift, scale).astype(BF16)

    pa = _dot(hb, win_ref[:, 0:4 * w])
    z = pa[:, w:2 * w] * pa[:, 2 * w:3 * w]
    zbuf[zoff:zoff + tm, :] = z
    cw = convw_ref[layer]
    conv = convb_ref[layer] + (zbuf[zoff - 2:zoff - 2 + tm, :] * cw[0:1]
                               + zbuf[zoff - 1:zoff - 1 + tm, :] * cw[1:2]
                               + z * cw[2:3])
    y_a = pa[:, 0:w] * conv * _silu(pa[:, 3 * w:4 * w])
    zbuf[zoff - (CONV_W - 1):zoff, :] = zbuf[zoff + tm - (CONV_W - 1):zoff + tm, :]

    pb = _dot(hb, win_ref[:, 4 * w:7 * w])
    u = _gelu(pb[:, 0:w])
    v = _layernorm(_gelu(pb[:, w:2 * w]), lng_ref[layer], lnb_ref[layer])
    vb = v.astype(BF16)
    gw = HEADS_PER_DOT * CHUNK
    t_idx = lax.broadcasted_iota(jnp.int32, (CHUNK, gw), 0)
    s_idx = lax.broadcasted_iota(jnp.int32, (CHUNK, gw), 1) % CHUNK
    causal = s_idx <= t_idx
    hd = w // SGU_HEADS
    gcols = HEADS_PER_DOT * hd
    lane_head = lax.broadcasted_iota(jnp.int32, (CHUNK, gcols), 1) // hd
    bias = sgub_ref[layer]
    mixed_rows = []
    for c in range(tm // CHUNK):
        vc = vb[c * CHUNK:(c + 1) * CHUNK, :]
        outs = []
        for g in range(SGU_HEADS // HEADS_PER_DOT):
            wg = jnp.where(causal, sguw_ref[:, g * gw:(g + 1) * gw], jnp.zeros((), BF16))
            vg = vc[:, g * gcols:(g + 1) * gcols]
            rhs = jnp.concatenate(
                [jnp.where(lane_head == hh, vg, jnp.zeros((), BF16)) for hh in range(HEADS_PER_DOT)], axis=0)
            outs.append(_dot(wg, rhs))
        mixed_rows.append(jnp.concatenate(outs, axis=1) + bias)
    mixed = jnp.concatenate(mixed_rows, axis=0)
    y_b = u * mixed * _silu(pb[:, 2 * w:3 * w])

    pc = _dot(hb, win_ref[:, 7 * w:9 * w])
    c_x = pc[:, 0:w]
    cbuf[coff:coff + tm, :] = c_x
    gd = w // len(POOL_WINDOWS)
    pos1 = lax.broadcasted_iota(jnp.int32, (tm, gd), 0) + (t * tm + 1)
    yc_groups = []
    for g, win in enumerate(POOL_WINDOWS):
        cols = slice(g * gd, (g + 1) * gd)
        s = c_x[:, cols]
        for j in range(1, win):
            s = s + cbuf[coff - j:coff - j + tm, cols]
        cnt = jnp.minimum(pos1, win).astype(F32)
        pm = s / cnt - c_x[:, cols]
        yc_groups.append(_dot(pm.astype(BF16), poolw_ref[g]))
    y_c = jnp.concatenate(yc_groups, axis=1)
    y_c = (y_c + poolb_ref[layer]) * pools_ref[layer] * _silu(pc[:, w:2 * w])

    @pl.when(t == last_t)
    def _():
        convo_ref[0] = z[tm - (CONV_W - 1):tm, :]
        poolo_ref[0] = cbuf[coff + tm - POOL_BUF:coff + tm, :]
        vo_ref[0] = v[tm - CHUNK:tm, :]

    cbuf[0:coff, :] = cbuf[tm:tm + coff, :]

    pg = _dot(hb, win_ref[:, 9 * w:9 * w + 3 * d])
    merged = (_sigmoid(pg[:, 0:d]) * _dot(y_a.astype(BF16), wbr_ref[0])
              + _sigmoid(pg[:, d:2 * d]) * _dot(y_b.astype(BF16), wbr_ref[1])
              + _sigmoid(pg[:, 2 * d:3 * d]) * _dot(y_c.astype(BF16), wbr_ref[2]))
    x_new = x + gate * _dot(merged.astype(BF16), wout_ref[...])
    if final:
        ms = jnp.mean(x_new * x_new, axis=-1, keepdims=True)
        x_new = x_new * lax.rsqrt(ms + EPS) * fg_ref[...]
    xo_ref[0] = x_new


def _prompt_call(layer, final, x, mod, small, win, sguw, poolw, wbr, wout):
    n, seq, d = x.shape
    w = d // 2
    tm = SEQ_TILE
    in_cols = win.shape[-1]
    resident = dict(pipeline_mode=pl.Buffered(1))
    whole = lambda a: pl.BlockSpec(a.shape, lambda b, t: (0,) * a.ndim, **resident)
    ng, convw, convb, lng, lnb, sgub, poolb, pools, fg = small
    in_specs = [
        pl.BlockSpec((1, tm, d), lambda b, t: (b, t, 0)),
        pl.BlockSpec((1, 1, 3 * d), lambda b, t: (b, 0, 0)),
        whole(ng),
        pl.BlockSpec((None, d, in_cols), lambda b, t: (layer, 0, 0), **resident),
        whole(convw), whole(convb), whole(lng), whole(lnb),
        pl.BlockSpec((None, CHUNK, SGU_HEADS * CHUNK), lambda b, t: (layer, 0, 0), **resident),
        whole(sgub),
        pl.BlockSpec((None,) + poolw.shape[1:], lambda b, t: (layer, 0, 0, 0), **resident),
        whole(poolb), whole(pools),
        pl.BlockSpec((None,) + wbr.shape[1:], lambda b, t: (layer, 0, 0, 0), **resident),
        pl.BlockSpec((None, d, d), lambda b, t: (layer, 0, 0), **resident),
        whole(fg),
    ]
    out_specs = [
        pl.BlockSpec((1, tm, d), lambda b, t: (b, t, 0)),
        pl.BlockSpec((1, CONV_W - 1, w), lambda b, t: (b, 0, 0)),
        pl.BlockSpec((1, POOL_BUF, w), lambda b, t: (b, 0, 0)),
        pl.BlockSpec((1, CHUNK, w), lambda b, t: (b, 0, 0)),
    ]
    out_shape = [
        jax.ShapeDtypeStruct((n, seq, d), F32),
        jax.ShapeDtypeStruct((n, CONV_W - 1, w), F32),
        jax.ShapeDtypeStruct((n, POOL_BUF, w), F32),
        jax.ShapeDtypeStruct((n, CHUNK, w), F32),
    ]
    return pl.pallas_call(
        functools.partial(_prompt_kernel, layer=layer, final=final),
        grid=(n, seq // tm),
        in_specs=in_specs,
        out_specs=out_specs,
        out_shape=out_shape,
        scratch_shapes=[
            pltpu.VMEM((SUBLANES + tm, w), F32),
            pltpu.VMEM((2 * SUBLANES + tm, w), F32),
        ],
        compiler_params=pltpu.CompilerParams(
            dimension_semantics=("arbitrary", "arbitrary"),
            vmem_limit_bytes=VMEM_LIMIT_BYTES),
        name=f"prompt_layer{layer}",
    )(x, mod, ng, win, convw, convb, lng, lnb, sguw, sgub, poolw, poolb, pools, wbr, wout, fg)


def _sample_kernel(x_ref, mod_ref, sconv_ref, spool_ref, ng_ref, win_ref, convw_ref, convb_ref, lng_ref,
                   lnb_ref, sguw_ref, sgub_ref, poolw_ref, poolb_ref, pools_ref, wbr_ref, wout_ref, fg_ref,
                   y_ref, convo_ref, poolo_ref, vo_ref, xs):
    l = pl.program_id(0)
    d = x_ref.shape[1]
    w = d // 2

    @pl.when(l == 0)
    def _():
        xs[...] = x_ref[...]

    x = xs[...]
    mod = mod_ref[0]
    shift, scale, gate = mod[:, 0:d], mod[:, d:2 * d], mod[:, 2 * d:3 * d]
    hb = _modulated_norm(x, ng_ref[0], shift, scale).astype(BF16)

    pa = _dot(hb, win_ref[0, :, 0:4 * w])
    z = pa[:, w:2 * w] * pa[:, 2 * w:3 * w]
    prev = sconv_ref[0]
    cw = convw_ref[0]
    conv = convb_ref[0] + (prev[:, 0:w] * cw[0:1] + prev[:, w:2 * w] * cw[1:2] + z * cw[2:3])
    y_a = pa[:, 0:w] * conv * _silu(pa[:, 3 * w:4 * w])
    convo_ref[0, :, 0:w] = prev[:, w:2 * w]
    convo_ref[0, :, w:2 * w] = z

    pb = _dot(hb, win_ref[0, :, 4 * w:7 * w])
    u = _gelu(pb[:, 0:w])
    v = _layernorm(_gelu(pb[:, w:2 * w]), lng_ref[0], lnb_ref[0])
    vo_ref[0] = v
    y_b = u * (v * sguw_ref[0] + sgub_ref[0]) * _silu(pb[:, 2 * w:3 * w])

    pc = _dot(hb, win_ref[0, :, 7 * w:9 * w])
    c_x = pc[:, 0:w]
    gd = w // len(POOL_WINDOWS)
    yc_groups = []
    for g, win in enumerate(POOL_WINDOWS):
        s = c_x[:, g * gd:(g + 1) * gd]
        for j in range(POOL_BUF - (win - 1), POOL_BUF):
            s = s + spool_ref[0, :, j * w + g * gd:j * w + (g + 1) * gd]
        pm = s / float(min(win, PAST_LEN + 1)) - c_x[:, g * gd:(g + 1) * gd]
        yc_groups.append(_dot(pm.astype(BF16), poolw_ref[0, g]))
    y_c = jnp.concatenate(yc_groups, axis=1)
    y_c = (y_c + poolb_ref[0]) * pools_ref[0] * _silu(pc[:, w:2 * w])
    poolo_ref[0, :, 0:(POOL_BUF - 1) * w] = spool_ref[0, :, w:POOL_BUF * w]
    poolo_ref[0, :, (POOL_BUF - 1) * w:POOL_BUF * w] = c_x

    pg = _dot(hb, win_ref[0, :, 9 * w:9 * w + 3 * d])
    merged = (_sigmoid(pg[:, 0:d]) * _dot(y_a.astype(BF16), wbr_ref[0, 0])
              + _sigmoid(pg[:, d:2 * d]) * _dot(y_b.astype(BF16), wbr_ref[0, 1])
              + _sigmoid(pg[:, 2 * d:3 * d]) * _dot(y_c.astype(BF16), wbr_ref[0, 2]))
    x_new = x + gate * _dot(merged.astype(BF16), wout_ref[0])
    xs[...] = x_new

    @pl.when(l == pl.num_programs(0) - 1)
    def _():
        ms = jnp.mean(x_new * x_new, axis=-1, keepdims=True)
        y_ref[...] = x_new * lax.rsqrt(ms + EPS) * fg_ref[...]


def _sample_call(x, mod, sconv, spool, small, sgu_w00, sgu_b0, win, poolw, wbr, wout):
    rows, d = x.shape
    w = d // 2
    depth = win.shape[0]
    ng, convw, convb, lng, lnb, _, poolb, pools, fg = small
    per_layer = lambda a, **kw: pl.BlockSpec((1,) + a.shape[1:], lambda l: (l,) + (0,) * (a.ndim - 1), **kw)
    in_specs = [
        pl.BlockSpec((rows, d), lambda l: (0, 0)),
        per_layer(mod), per_layer(sconv), per_layer(spool), per_layer(ng),
        per_layer(win, pipeline_mode=pl.Buffered(1)),
        per_layer(convw), per_layer(convb), per_layer(lng), per_layer(lnb),
        per_layer(sgu_w00), per_layer(sgu_b0), per_layer(poolw), per_layer(poolb), per_layer(pools),
        per_layer(wbr), per_layer(wout),
        pl.BlockSpec(fg.shape, lambda l: (0, 0)),
    ]
    out_shape = [
        jax.ShapeDtypeStruct((rows, d), F32),
        jax.ShapeDtypeStruct(sconv.shape, F32),
        jax.ShapeDtypeStruct(spool.shape, F32),
        jax.ShapeDtypeStruct((depth, rows, w), F32),
    ]
    out_specs = [
        pl.BlockSpec((rows, d), lambda l: (0, 0)),
        per_layer(sconv), per_layer(spool),
        pl.BlockSpec((1, rows, w), lambda l: (l, 0, 0)),
    ]
    return pl.pallas_call(
        _sample_kernel,
        grid=(depth,),
        in_specs=in_specs,
        out_specs=out_specs,
        out_shape=out_shape,
        scratch_shapes=[pltpu.VMEM((rows, d), F32)],
        compiler_params=pltpu.CompilerParams(
            dimension_semantics=("arbitrary",),
            vmem_limit_bytes=VMEM_LIMIT_BYTES),
        name="sample_layers",
    )(x, mod, sconv, spool, ng, win, convw, convb, lng, lnb, sgu_w00, sgu_b0, poolw, poolb, pools,
      wbr, wout, fg)


def kernel(x_prompt, x_sample, c_prompt, c_sample, state_conv, state_pool, w_ada, b_ada, norm_g, w_in, conv_w,
           conv_b, lnv_g, lnv_b, sgu_w, sgu_b, pool_w, pool_b, pool_scale, w_branch, w_out, final_g):
    n, seq, d = x_prompt.shape
    rows = x_sample.shape[0]
    depth = w_in.shape[0]
    w = d // 2
    hd = w // SGU_HEADS
    assert x_sample.shape[1] == 1 and seq % SEQ_TILE == 0 and SEQ_TILE % CHUNK == 0

    pad = 2 * SUBLANES
    c_all = jnp.concatenate([jnp.pad(c_prompt, ((0, pad - n), (0, 0))), c_sample], axis=0)
    mod = _mod_call(c_all, w_ada, b_ada)
    mod_p = mod[:, :n].reshape(depth, n, 1, 3 * d)
    mod_s = mod[:, pad:pad + rows]

    row = lambda a: a.reshape(depth, 1, a.shape[-1])
    sgub_full = jnp.repeat(jnp.swapaxes(sgu_b, 1, 2), hd, axis=2)
    small = (row(norm_g), conv_w, row(conv_b), row(lnv_g), row(lnv_b), sgub_full, row(pool_b),
             row(pool_scale), final_g.reshape(1, d))
    win = w_in.astype(BF16)
    wbr = w_branch.astype(BF16)
    wout = w_out.astype(BF16)
    poolw = pool_w.astype(BF16)
    sguw = jnp.transpose(sgu_w, (0, 2, 1, 3)).reshape(depth, CHUNK, SGU_HEADS * CHUNK).astype(BF16)
    sgu_w00 = jnp.repeat(sgu_w[:, :, 0, 0], hd, axis=1).reshape(depth, 1, w)
    sgu_b0 = jnp.repeat(sgu_b[:, :, 0], hd, axis=1).reshape(depth, 1, w)

    xp = x_prompt
    conv_p, pool_p, v_p = [], [], []
    for l in range(depth):
        xp, cp, pp, vp = _prompt_call(l, l == depth - 1, xp, mod_p[l], small, win, sguw, poolw, wbr, wout)
        conv_p.append(cp); pool_p.append(pp); v_p.append(vp)

    y_s, conv_s, pool_s, v_s = _sample_call(
        x_sample.reshape(rows, d), mod_s,
        state_conv.reshape(depth, rows, (CONV_W - 1) * w), state_pool.reshape(depth, rows, POOL_BUF * w),
        small, sgu_w00, sgu_b0, win, poolw, wbr, wout)

    return (xp, y_s.reshape(rows, 1, d), jnp.stack(conv_p), conv_s.reshape(depth, rows, CONV_W - 1, w),
            jnp.stack(pool_p), pool_s.reshape(depth, rows, POOL_BUF, w), jnp.stack(v_p),
            v_s.reshape(depth, rows, 1, w))
```

```python
import functools
import math

import jax
import jax.numpy as jnp
from jax import lax
from jax.experimental import pallas as pl
from jax.experimental.pallas import tpu as pltpu

F32 = jnp.float32
BF16 = jnp.bfloat16

CONV_W = 3
CHUNK = 128
SGU_HEADS = 8
POOL_WINDOWS = (2, 4, 8, 16)
POOL_BUF = max(POOL_WINDOWS) - 1
PAST_LEN = 16384
EPS = 1e-6

SUBLANES = 8
SEQ_TILE = 512
HEADS_PER_DOT = 4
VMEM_LIMIT_BYTES = 52 * 1024 * 1024


def _sigmoid(x):
    return 0.5 * (1.0 + jnp.tanh(0.5 * x))


def _silu(x):
    return x * _sigmoid(x)


def _gelu(x):
    c = math.sqrt(2.0 / math.pi)
    return 0.5 * x * (1.0 + jnp.tanh(c * (x + 0.044715 * (x * x * x))))


def _dot(a, b):
    return jnp.dot(a, b, preferred_element_type=F32)


def _modulated_norm(x, norm_g, shift, scale):
    ms = jnp.mean(x * x, axis=-1, keepdims=True)
    return (x * lax.rsqrt(ms + EPS)) * (norm_g * (1.0 + scale)) + shift


def _layernorm(x, g, b):
    mu = jnp.mean(x, axis=-1, keepdims=True)
    d = x - mu
    var = jnp.mean(d * d, axis=-1, keepdims=True)
    return d * lax.rsqrt(var + EPS) * g + b


def _mod_kernel(c_ref, w_ref, b_ref, o_ref):
    s = _silu(c_ref[...]).astype(BF16)
    o_ref[0] = _dot(s, w_ref[0].astype(BF16)) + b_ref[0]


def _mod_call(c_all, w_ada, b_ada):
    depth, d, d3 = w_ada.shape
    rows = c_all.shape[0]
    col_block = d3 // 2
    return pl.pallas_call(
        _mod_kernel,
        grid=(depth, d3 // col_block),
        in_specs=[
            pl.BlockSpec((rows, d), lambda l, j: (0, 0)),
            pl.BlockSpec((1, d, col_block), lambda l, j: (l, 0, j)),
            pl.BlockSpec((1, 1, col_block), lambda l, j: (l, 0, j)),
        ],
        out_specs=pl.BlockSpec((1, rows, col_block), lambda l, j: (l, 0, j)),
        out_shape=jax.ShapeDtypeStruct((depth, rows, d3), F32),
        compiler_params=pltpu.CompilerParams(
            dimension_semantics=("arbitrary", "arbitrary"),
            vmem_limit_bytes=VMEM_LIMIT_BYTES),
        name="adaln_mod",
    )(c_all, w_ada, b_ada.reshape(depth, 1, d3))


def _prompt_kernel(x_ref, mod_ref, ng_ref, win_ref, convw_ref, convb_ref, lng_ref, lnb_ref,
                   sguw_ref, sgub_ref, poolw_ref, poolb_ref, pools_ref, wbr_ref, wout_ref, fg_ref,
                   xo_ref, convo_ref, poolo_ref, vo_ref,
                   zbuf, cbuf, *, layer, final):
    t = pl.program_id(1)
    last_t = pl.num_programs(1) - 1
    tm, d = x_ref.shape[1], x_ref.shape[2]
    w = d // 2
    zoff = SUBLANES
    coff = 2 * SUBLANES

    @pl.when(t == 0)
    def _():
        zbuf[0:zoff, :] = jnp.zeros((zoff, w), F32)
        cbuf[0:coff, :] = jnp.zeros((coff, w), F32)

    x = x_ref[0]
    mod = mod_ref[0]
    shift, scale, gate = mod[:, 0:d], mod[:, d:2 * d], mod[:, 2 * d:3 * d]
    hb = _modulated_norm(x, ng_ref[layer], shift, scale).astype(BF16)

    pa = _dot(hb, win_ref[:, 0:4 * w])
    z = pa[:, w:2 * w] * pa[:, 2 * w:3 * w]
    zbuf[zoff:zoff + tm, :] = z
    cw = convw_ref[layer]
    conv = convb_ref[layer] + (zbuf[zoff - 2:zoff - 2 + tm, :] * cw[0:1]
                               + zbuf[zoff - 1:zoff - 1 + tm, :] * cw[1:2]
                               + z * cw[2:3])
    y_a = pa[:, 0:w] * conv * _silu(pa[:, 3 * w:4 * w])
    zbuf[zoff - (CONV_W - 1):zoff, :] = zbuf[zoff + tm - (CONV_W - 1):zoff + tm, :]

    pb = _dot(hb, win_ref[:, 4 * w:7 * w])
    u = _gelu(pb[:, 0:w])
    v = _layernorm(_gelu(pb[:, w:2 * w]), lng_ref[layer], lnb_ref[layer])
    vb = v.astype(BF16)
    gw = HEADS_PER_DOT * CHUNK
    t_idx = lax.broadcasted_iota(jnp.int32, (CHUNK, gw), 0)
    s_idx = lax.broadcasted_iota(jnp.int32, (CHUNK, gw), 1) % CHUNK
    causal = s_idx <= t_idx
    hd = w // SGU_HEADS
    gcols = HEADS_PER_DOT * hd
    lane_head = lax.broadcasted_iota(jnp.int32, (CHUNK, gcols), 1) // hd
    bias = sgub_ref[layer]
    mixed_rows = []
    for c in range(tm // CHUNK):
        vc = vb[c * CHUNK:(c + 1) * CHUNK, :]
        outs = []
        for g in range(SGU_HEADS // HEADS_PER_DOT):
            wg = jnp.where(causal, sguw_ref[:, g * gw:(g + 1) * gw], jnp.zeros((), BF16))
            vg = vc[:, g * gcols:(g + 1) * gcols]
            rhs = jnp.concatenate(
                [jnp.where(lane_head == hh, vg, jnp.zeros((), BF16)) for hh in range(HEADS_PER_DOT)], axis=0)
            outs.append(_dot(wg, rhs))
        mixed_rows.append(jnp.concatenate(outs, axis=1) + bias)
    mixed = jnp.concatenate(mixed_rows, axis=0)
    y_b = u * mixed * _silu(pb[:, 2 * w:3 * w])

    pc = _dot(hb, win_ref[:, 7 * w:9 * w])
    c_x = pc[:, 0:w]
    cbuf[coff:coff + tm, :] = c_x
    gd = w // len(POOL_WINDOWS)
    pos1 = lax.broadcasted_iota(jnp.int32, (tm, gd), 0) + (t * tm + 1)
    yc_groups = []
    for g, win in enumerate(POOL_WINDOWS):
        cols = slice(g * gd, (g + 1) * gd)
        s = c_x[:, cols]
        for j in range(1, win):
            s = s + cbuf[coff - j:coff - j + tm, cols]
        cnt = jnp.minimum(pos1, win).astype(F32)
        pm = s / cnt - c_x[:, cols]
        yc_groups.append(_dot(pm.astype(BF16), poolw_ref[g]))
    y_c = jnp.concatenate(yc_groups, axis=1)
    y_c = (y_c + poolb_ref[layer]) * pools_ref[layer] * _silu(pc[:, w:2 * w])

    @pl.when(t == last_t)
    def _():
        convo_ref[0] = z[tm - (CONV_W - 1):tm, :]
        poolo_ref[0] = cbuf[coff + tm - POOL_BUF:coff + tm, :]
        vo_ref[0] = v[tm - CHUNK:tm, :]

    cbuf[0:coff, :] = cbuf[tm:tm + coff, :]

    pg = _dot(hb, win_ref[:, 9 * w:9 * w + 3 * d])
    merged = (_sigmoid(pg[:, 0:d]) * _dot(y_a.astype(BF16), wbr_ref[0])
              + _sigmoid(pg[:, d:2 * d]) * _dot(y_b.astype(BF16), wbr_ref[1])
              + _sigmoid(pg[:, 2 * d:3 * d]) * _dot(y_c.astype(BF16), wbr_ref[2]))
    x_new = x + gate * _dot(merged.astype(BF16), wout_ref[...])
    if final:
        ms = jnp.mean(x_new * x_new, axis=-1, keepdims=True)
        x_new = x_new * lax.rsqrt(ms + EPS) * fg_ref[...]
    xo_ref[0] = x_new


def _prompt_call(layer, final, x, mod, small, win, sguw, poolw, wbr, wout):
    n, seq, d = x.shape
    w = d // 2
    tm = SEQ_TILE
    in_cols = win.shape[-1]
    resident = dict(pipeline_mode=pl.Buffered(1))
    whole = lambda a: pl.BlockSpec(a.shape, lambda b, t: (0,) * a.ndim, **resident)
    ng, convw, convb, lng, lnb, sgub, poolb, pools, fg = small
    in_specs = [
        pl.BlockSpec((1, tm, d), lambda b, t: (b, t, 0)),
        pl.BlockSpec((1, 1, 3 * d), lambda b, t: (b, 0, 0)),
        whole(ng),
        pl.BlockSpec((None, d, in_cols), lambda b, t: (layer, 0, 0), **resident),
        whole(convw), whole(convb), whole(lng), whole(lnb),
        pl.BlockSpec((None, CHUNK, SGU_HEADS * CHUNK), lambda b, t: (layer, 0, 0), **resident),
        whole(sgub),
        pl.BlockSpec((None,) + poolw.shape[1:], lambda b, t: (layer, 0, 0, 0), **resident),
        whole(poolb), whole(pools),
        pl.BlockSpec((None,) + wbr.shape[1:], lambda b, t: (layer, 0, 0, 0), **resident),
        pl.BlockSpec((None, d, d), lambda b, t: (layer, 0, 0), **resident),
        whole(fg),
    ]
    out_specs = [
        pl.BlockSpec((1, tm, d), lambda b, t: (b, t, 0)),
        pl.BlockSpec((1, CONV_W - 1, w), lambda b, t: (b, 0, 0)),
        pl.BlockSpec((1, POOL_BUF, w), lambda b, t: (b, 0, 0)),
        pl.BlockSpec((1, CHUNK, w), lambda b, t: (b, 0, 0)),
    ]
    out_shape = [
        jax.ShapeDtypeStruct((n, seq, d), F32),
        jax.ShapeDtypeStruct((n, CONV_W - 1, w), F32),
        jax.ShapeDtypeStruct((n, POOL_BUF, w), F32),
        jax.ShapeDtypeStruct((n, CHUNK, w), F32),
    ]
    return pl.pallas_call(
        functools.partial(_prompt_kernel, layer=layer, final=final),
        grid=(n, seq // tm),
        in_specs=in_specs,
        out_specs=out_specs,
        out_shape=out_shape,
        scratch_shapes=[
            pltpu.VMEM((SUBLANES + tm, w), F32),
            pltpu.VMEM((2 * SUBLANES + tm, w), F32),
        ],
        compiler_params=pltpu.CompilerParams(
            dimension_semantics=("arbitrary", "arbitrary"),
            vmem_limit_bytes=VMEM_LIMIT_BYTES),
        name=f"prompt_layer{layer}",
    )(x, mod, ng, win, convw, convb, lng, lnb, sguw, sgub, poolw, poolb, pools, wbr, wout, fg)


def _sample_kernel(x_ref, mod_ref, sconv_ref, spool_ref, ng_ref, win_ref, convw_ref, convb_ref, lng_ref,
                   lnb_ref, sguw_ref, sgub_ref, poolw_ref, poolb_ref, pools_ref, wbr_ref, wout_ref, fg_ref,
                   y_ref, convo_ref, poolo_ref, vo_ref, xs):
    l = pl.program_id(0)
    d = x_ref.shape[1]
    w = d // 2

    @pl.when(l == 0)
    def _():
        xs[...] = x_ref[...]

    x = xs[...]
    mod = mod_ref[0]
    shift, scale, gate = mod[:, 0:d], mod[:, d:2 * d], mod[:, 2 * d:3 * d]
    hb = _modulated_norm(x, ng_ref[0], shift, scale).astype(BF16)

    pa = _dot(hb, win_ref[0, :, 0:4 * w])
    z = pa[:, w:2 * w] * pa[:, 2 * w:3 * w]
    prev = sconv_ref[0]
    cw = convw_ref[0]
    conv = convb_ref[0] + (prev[:, 0:w] * cw[0:1] + prev[:, w:2 * w] * cw[1:2] + z * cw[2:3])
    y_a = pa[:, 0:w] * conv * _silu(pa[:, 3 * w:4 * w])
    convo_ref[0, :, 0:w] = prev[:, w:2 * w]
    convo_ref[0, :, w:2 * w] = z

    pb = _dot(hb, win_ref[0, :, 4 * w:7 * w])
    u = _gelu(pb[:, 0:w])
    v = _layernorm(_gelu(pb[:, w:2 * w]), lng_ref[0], lnb_ref[0])
    vo_ref[0] = v
    y_b = u * (v * sguw_ref[0] + sgub_ref[0]) * _silu(pb[:, 2 * w:3 * w])

    pc = _dot(hb, win_ref[0, :, 7 * w:9 * w])
    c_x = pc[:, 0:w]
    gd = w // len(POOL_WINDOWS)
    yc_groups = []
    for g, win in enumerate(POOL_WINDOWS):
        s = c_x[:, g * gd:(g + 1) * gd]
        for j in range(POOL_BUF - (win - 1), POOL_BUF):
            s = s + spool_ref[0, :, j * w + g * gd:j * w + (g + 1) * gd]
        pm = s / float(min(win, PAST_LEN + 1)) - c_x[:, g * gd:(g + 1) * gd]
        yc_groups.append(_dot(pm.astype(BF16), poolw_ref[0, g]))
    y_c = jnp.concatenate(yc_groups, axis=1)
    y_c = (y_c + poolb_ref[0]) * pools_ref[0] * _silu(pc[:, w:2 * w])
    poolo_ref[0, :, 0:(POOL_BUF - 1) * w] = spool_ref[0, :, w:POOL_BUF * w]
    poolo_ref[0, :, (POOL_BUF - 1) * w:POOL_BUF * w] = c_x

    pg = _dot(hb, win_ref[0, :, 9 * w:9 * w + 3 * d])
    merged = (_sigmoid(pg[:, 0:d]) * _dot(y_a.astype(BF16), wbr_ref[0, 0])
              + _sigmoid(pg[:, d:2 * d]) * _dot(y_b.astype(BF16), wbr_ref[0, 1])
              + _sigmoid(pg[:, 2 * d:3 * d]) * _dot(y_c.astype(BF16), wbr_ref[0, 2]))
    x_new = x + gate * _dot(merged.astype(BF16), wout_ref[0])
    xs[...] = x_new

    @pl.when(l == pl.num_programs(0) - 1)
    def _():
        ms = jnp.mean(x_new * x_new, axis=-1, keepdims=True)
        y_ref[...] = x_new * lax.rsqrt(ms + EPS) * fg_ref[...]


def _sample_call(x, mod, sconv, spool, small, sgu_w00, sgu_b0, win, poolw, wbr, wout):
    rows, d = x.shape
    w = d // 2
    depth = win.shape[0]
    ng, convw, convb, lng, lnb, _, poolb, pools, fg = small
    per_layer = lambda a, **kw: pl.BlockSpec((1,) + a.shape[1:], lambda l: (l,) + (0,) * (a.ndim - 1), **kw)
    in_specs = [
        pl.BlockSpec((rows, d), lambda l: (0, 0)),
        per_layer(mod), per_layer(sconv), per_layer(spool), per_layer(ng),
        per_layer(win, pipeline_mode=pl.Buffered(1)),
        per_layer(convw), per_layer(convb), per_layer(lng), per_layer(lnb),
        per_layer(sgu_w00), per_layer(sgu_b0), per_layer(poolw), per_layer(poolb), per_layer(pools),
        per_layer(wbr), per_layer(wout),
        pl.BlockSpec(fg.shape, lambda l: (0, 0)),
    ]
    out_shape = [
        jax.ShapeDtypeStruct((rows, d), F32),
        jax.ShapeDtypeStruct(sconv.shape, F32),
        jax.ShapeDtypeStruct(spool.shape, F32),
        jax.ShapeDtypeStruct((depth, rows, w), F32),
    ]
    out_specs = [
        pl.BlockSpec((rows, d), lambda l: (0, 0)),
        per_layer(sconv), per_layer(spool),
        pl.BlockSpec((1, rows, w), lambda l: (l, 0, 0)),
    ]
    return pl.pallas_call(
        _sample_kernel,
        grid=(depth,),
        in_specs=in_specs,
        out_specs=out_specs,
        out_shape=out_shape,
        scratch_shapes=[pltpu.VMEM((rows, d), F32)],
        compiler_params=pltpu.CompilerParams(
            dimension_semantics=("arbitrary",),
            vmem_limit_bytes=VMEM_LIMIT_BYTES),
        name="sample_layers",
    )(x, mod, sconv, spool, ng, win, convw, convb, lng, lnb, sgu_w00, sgu_b0, poolw, poolb, pools,
      wbr, wout, fg)


def kernel(x_prompt, x_sample, c_prompt, c_sample, state_conv, state_pool, w_ada, b_ada, norm_g, w_in, conv_w,
           conv_b, lnv_g, lnv_b, sgu_w, sgu_b, pool_w, pool_b, pool_scale, w_branch, w_out, final_g):
    n, seq, d = x_prompt.shape
    rows = x_sample.shape[0]
    depth = w_in.shape[0]
    w = d // 2
    hd = w // SGU_HEADS
    assert x_sample.shape[1] == 1 and seq % SEQ_TILE == 0 and SEQ_TILE % CHUNK == 0

    pad = 2 * SUBLANES
    c_all = jnp.concatenate([jnp.pad(c_prompt, ((0, pad - n), (0, 0))), c_sample], axis=0)
    mod = _mod_call(c_all, w_ada, b_ada)
    mod_p = mod[:, :n].reshape(depth, n, 1, 3 * d)
    mod_s = mod[:, pad:pad + rows]

    row = lambda a: a.reshape(depth, 1, a.shape[-1])
    sgub_full = jnp.repeat(jnp.swapaxes(sgu_b, 1, 2), hd, axis=2)
    small = (row(norm_g), conv_w, row(conv_b), row(lnv_g), row(lnv_b), sgub_full, row(pool_b),
             row(pool_scale), final_g.reshape(1, d))
    win = w_in.astype(BF16)
    wbr = w_branch.astype(BF16)
    wout = w_out.astype(BF16)
    poolw = pool_w.astype(BF16)
    sguw = jnp.transpose(sgu_w, (0, 2, 1, 3)).reshape(depth, CHUNK, SGU_HEADS * CHUNK).astype(BF16)
    sgu_w00 = jnp.repeat(sgu_w[:, :, 0, 0], hd, axis=1).reshape(depth, 1, w)
    sgu_b0 = jnp.repeat(sgu_b[:, :, 0], hd, axis=1).reshape(depth, 1, w)

    xp = x_prompt
    conv_p, pool_p, v_p = [], [], []
    for l in range(depth):
        xp, cp, pp, vp = _prompt_call(l, l == depth - 1, xp, mod_p[l], small, win, sguw, poolw, wbr, wout)
        conv_p.append(cp); pool_p.append(pp); v_p.append(vp)

    y_s, conv_s, pool_s, v_s = _sample_call(
        x_sample.reshape(rows, d), mod_s,
        state_conv.reshape(depth, rows, (CONV_W - 1) * w), state_pool.reshape(depth, rows, POOL_BUF * w),
        small, sgu_w00, sgu_b0, win, poolw, wbr, wout)

    return (xp, y_s.reshape(rows, 1, d), jnp.stack(conv_p), conv_s.reshape(depth, rows, CONV_W - 1, w),
            jnp.stack(pool_p), pool_s.reshape(depth, rows, POOL_BUF, w), jnp.stack(v_p),
            v_s.reshape(depth, rows, 1, w))
```

```python
import functools
import math

import jax
import jax.numpy as jnp
from jax import lax
from jax.experimental import pallas as pl
from jax.experimental.pallas import tpu as pltpu

F32 = jnp.float32
BF16 = jnp.bfloat16

CONV_W = 3
CHUNK = 128
SGU_HEADS = 8
POOL_WINDOWS = (2, 4, 8, 16)
POOL_BUF = max(POOL_WINDOWS) - 1
PAST_LEN = 16384
EPS = 1e-6

SUBLANES = 8
SEQ_TILE = 512
HEADS_PER_DOT = 4
VMEM_LIMIT_BYTES = 52 * 1024 * 1024


def _sigmoid(x):
    return 0.5 * (1.0 + jnp.tanh(0.5 * x))


def _silu(x):
    return x * _sigmoid(x)


def _gelu(x):
    c = math.sqrt(2.0 / math.pi)
    return 0.5 * x * (1.0 + jnp.tanh(c * (x + 0.044715 * (x * x * x))))


def _shift_rows(x, k):
    return pltpu.roll(x, k, axis=0)


def _dot(a, b):
    return jnp.dot(a, b, preferred_element_type=F32)


def _modulated_norm(x, norm_g, shift, scale):
    ms = jnp.mean(x * x, axis=-1, keepdims=True)
    return (x * lax.rsqrt(ms + EPS)) * (norm_g * (1.0 + scale)) + shift


def _layernorm(x, g, b):
    mu = jnp.mean(x, axis=-1, keepdims=True)
    d = x - mu
    var = jnp.mean(d * d, axis=-1, keepdims=True)
    return d * lax.rsqrt(var + EPS) * g + b


def _mod_kernel(c_ref, w_ref, b_ref, o_ref):
    s = _silu(c_ref[...]).astype(BF16)
    o_ref[0] = _dot(s, w_ref[0].astype(BF16)) + b_ref[0]


def _mod_call(c_all, w_ada, b_ada):
    depth, d, d3 = w_ada.shape
    rows = c_all.shape[0]
    col_block = d3 // 2
    return pl.pallas_call(
        _mod_kernel,
        grid=(depth, d3 // col_block),
        in_specs=[
            pl.BlockSpec((rows, d), lambda l, j: (0, 0)),
            pl.BlockSpec((1, d, col_block), lambda l, j: (l, 0, j)),
            pl.BlockSpec((1, 1, col_block), lambda l, j: (l, 0, j)),
        ],
        out_specs=pl.BlockSpec((1, rows, col_block), lambda l, j: (l, 0, j)),
        out_shape=jax.ShapeDtypeStruct((depth, rows, d3), F32),
        compiler_params=pltpu.CompilerParams(
            dimension_semantics=("arbitrary", "arbitrary"),
            vmem_limit_bytes=VMEM_LIMIT_BYTES),
        name="adaln_mod",
    )(c_all, w_ada, b_ada.reshape(depth, 1, d3))


def _prompt_kernel(x_ref, mod_ref, ng_ref, win_ref, convw_ref, convb_ref, lng_ref, lnb_ref,
                   sguw_ref, sgub_ref, poolw_ref, poolb_ref, pools_ref, wbr_ref, wout_ref, fg_ref,
                   xo_ref, convo_ref, poolo_ref, vo_ref,
                   zbuf, cbuf, *, layer, final):
    t = pl.program_id(1)
    tm, d = x_ref.shape[1], x_ref.shape[2]
    w = d // 2
    zoff = SUBLANES
    coff = 2 * SUBLANES

    @pl.when(t == 0)
    def _():
        zbuf[0:zoff, :] = jnp.zeros((zoff, w), F32)
        cbuf[0:coff, :] = jnp.zeros((coff, w), F32)

    x = x_ref[0]
    mod = mod_ref[0]
    shift, scale, gate = mod[:, 0:d], mod[:, d:2 * d], mod[:, 2 * d:3 * d]
    hb = _modulated_norm(x, ng_ref[layer], shift, scale).astype(BF16)

    pa = _dot(hb, win_ref[:, 0:4 * w])
    pb = _dot(hb, win_ref[:, 4 * w:7 * w])
    pc = _dot(hb, win_ref[:, 7 * w:9 * w])
    pg = _dot(hb, win_ref[:, 9 * w:9 * w + 3 * d])

    z = pa[:, w:2 * w] * pa[:, 2 * w:3 * w]
    zbuf[zoff:zoff + tm, :] = z
    zfull = zbuf[...]
    cw = convw_ref[layer]
    conv = convb_ref[layer] + (_shift_rows(zfull, 2)[zoff:, :] * cw[0:1]
                               + _shift_rows(zfull, 1)[zoff:, :] * cw[1:2]
                               + z * cw[2:3])
    y_a = pa[:, 0:w] * conv * _silu(pa[:, 3 * w:4 * w])
    zbuf[0:zoff, :] = zbuf[tm:tm + zoff, :]

    u = _gelu(pb[:, 0:w])
    v = _layernorm(_gelu(pb[:, w:2 * w]), lng_ref[layer], lnb_ref[layer])
    vb = v.astype(BF16)
    gw = HEADS_PER_DOT * CHUNK
    t_idx = lax.broadcasted_iota(jnp.int32, (CHUNK, gw), 0)
    s_idx = lax.broadcasted_iota(jnp.int32, (CHUNK, gw), 1) % CHUNK
    causal = s_idx <= t_idx
    hd = w // SGU_HEADS
    gcols = HEADS_PER_DOT * hd
    lane_head = lax.broadcasted_iota(jnp.int32, (CHUNK, gcols), 1) // hd
    bias = sgub_ref[layer]
    mixed_rows = []
    for c in range(tm // CHUNK):
        vc = vb[c * CHUNK:(c + 1) * CHUNK, :]
        outs = []
        for g in range(SGU_HEADS // HEADS_PER_DOT):
            wg = jnp.where(causal, sguw_ref[:, g * gw:(g + 1) * gw], jnp.zeros((), BF16))
            vg = vc[:, g * gcols:(g + 1) * gcols]
            rhs = jnp.concatenate(
                [jnp.where(lane_head == hh, vg, jnp.zeros((), BF16)) for hh in range(HEADS_PER_DOT)], axis=0)
            outs.append(_dot(wg, rhs))
        mixed_rows.append(jnp.concatenate(outs, axis=1) + bias)
    mixed = jnp.concatenate(mixed_rows, axis=0)
    y_b = u * mixed * _silu(pb[:, 2 * w:3 * w])

    c_x = pc[:, 0:w]
    cbuf[coff:coff + tm, :] = c_x
    gd = w // len(POOL_WINDOWS)
    pos1 = lax.broadcasted_iota(jnp.int32, (tm, gd), 0) + (t * tm + 1)
    yc_groups = []
    for g, win in enumerate(POOL_WINDOWS):
        cols = slice(g * gd, (g + 1) * gd)
        s = cbuf[:, cols]
        k = 1
        while k < win:
            s = s + _shift_rows(s, k)
            k *= 2
        s = s[coff:, :]
        cnt = jnp.minimum(pos1, win).astype(F32)
        pm = s / cnt - c_x[:, cols]
        yc_groups.append(_dot(pm.astype(BF16), poolw_ref[g]))
    y_c = jnp.concatenate(yc_groups, axis=1)
    y_c = (y_c + poolb_ref[layer]) * pools_ref[layer] * _silu(pc[:, w:2 * w])

    convo_ref[0] = z[tm - (CONV_W - 1):tm, :]
    poolo_ref[0] = cbuf[coff + tm - POOL_BUF:coff + tm, :]
    vo_ref[0] = v[tm - CHUNK:tm, :]
    cbuf[0:coff, :] = cbuf[tm:tm + coff, :]

    merged = (_sigmoid(pg[:, 0:d]) * _dot(y_a.astype(BF16), wbr_ref[0])
              + _sigmoid(pg[:, d:2 * d]) * _dot(y_b.astype(BF16), wbr_ref[1])
              + _sigmoid(pg[:, 2 * d:3 * d]) * _dot(y_c.astype(BF16), wbr_ref[2]))
    x_new = x + gate * _dot(merged.astype(BF16), wout_ref[...])
    if final:
        ms = jnp.mean(x_new * x_new, axis=-1, keepdims=True)
        x_new = x_new * lax.rsqrt(ms + EPS) * fg_ref[...]
    xo_ref[0] = x_new


def _prompt_call(layer, final, x, mod, small, win, sguw, poolw, wbr, wout):
    n, seq, d = x.shape
    w = d // 2
    tm = SEQ_TILE
    in_cols = win.shape[-1]
    resident = dict(pipeline_mode=pl.Buffered(1))
    whole = lambda a: pl.BlockSpec(a.shape, lambda b, t: (0,) * a.ndim, **resident)
    ng, convw, convb, lng, lnb, sgub, poolb, pools, fg = small
    in_specs = [
        pl.BlockSpec((1, tm, d), lambda b, t: (b, t, 0)),
        pl.BlockSpec((1, 1, 3 * d), lambda b, t: (b, 0, 0)),
        whole(ng),
        pl.BlockSpec((None, d, in_cols), lambda b, t: (layer, 0, 0), **resident),
        whole(convw), whole(convb), whole(lng), whole(lnb),
        pl.BlockSpec((None, CHUNK, SGU_HEADS * CHUNK), lambda b, t: (layer, 0, 0), **resident),
        whole(sgub),
        pl.BlockSpec((None,) + poolw.shape[1:], lambda b, t: (layer, 0, 0, 0), **resident),
        whole(poolb), whole(pools),
        pl.BlockSpec((None,) + wbr.shape[1:], lambda b, t: (layer, 0, 0, 0), **resident),
        pl.BlockSpec((None, d, d), lambda b, t: (layer, 0, 0), **resident),
        whole(fg),
    ]
    out_specs = [
        pl.BlockSpec((1, tm, d), lambda b, t: (b, t, 0)),
        pl.BlockSpec((1, CONV_W - 1, w), lambda b, t: (b, 0, 0)),
        pl.BlockSpec((1, POOL_BUF, w), lambda b, t: (b, 0, 0)),
        pl.BlockSpec((1, CHUNK, w), lambda b, t: (b, 0, 0)),
    ]
    out_shape = [
        jax.ShapeDtypeStruct((n, seq, d), F32),
        jax.ShapeDtypeStruct((n, CONV_W - 1, w), F32),
        jax.ShapeDtypeStruct((n, POOL_BUF, w), F32),
        jax.ShapeDtypeStruct((n, CHUNK, w), F32),
    ]
    return pl.pallas_call(
        functools.partial(_prompt_kernel, layer=layer, final=final),
        grid=(n, seq // tm),
        in_specs=in_specs,
        out_specs=out_specs,
        out_shape=out_shape,
        scratch_shapes=[
            pltpu.VMEM((SUBLANES + tm, w), F32),
            pltpu.VMEM((2 * SUBLANES + tm, w), F32),
        ],
        compiler_params=pltpu.CompilerParams(
            dimension_semantics=("arbitrary", "arbitrary"),
            vmem_limit_bytes=VMEM_LIMIT_BYTES),
        name=f"prompt_layer{layer}",
    )(x, mod, ng, win, convw, convb, lng, lnb, sguw, sgub, poolw, poolb, pools, wbr, wout, fg)


def _sample_kernel(x_ref, mod_ref, sconv_ref, spool_ref, ng_ref, win_ref, convw_ref, convb_ref, lng_ref,
                   lnb_ref, sguw_ref, sgub_ref, poolw_ref, poolb_ref, pools_ref, wbr_ref, wout_ref, fg_ref,
                   y_ref, convo_ref, poolo_ref, vo_ref, xs):
    l = pl.program_id(0)
    d = x_ref.shape[1]
    w = d // 2

    @pl.when(l == 0)
    def _():
        xs[...] = x_ref[...]

    x = xs[...]
    mod = mod_ref[0]
    shift, scale, gate = mod[:, 0:d], mod[:, d:2 * d], mod[:, 2 * d:3 * d]
    hb = _modulated_norm(x, ng_ref[0], shift, scale).astype(BF16)

    pa = _dot(hb, win_ref[0, :, 0:4 * w])
    z = pa[:, w:2 * w] * pa[:, 2 * w:3 * w]
    prev = sconv_ref[0]
    cw = convw_ref[0]
    conv = convb_ref[0] + (prev[:, 0:w] * cw[0:1] + prev[:, w:2 * w] * cw[1:2] + z * cw[2:3])
    y_a = pa[:, 0:w] * conv * _silu(pa[:, 3 * w:4 * w])
    convo_ref[0, :, 0:w] = prev[:, w:2 * w]
    convo_ref[0, :, w:2 * w] = z

    pb = _dot(hb, win_ref[0, :, 4 * w:7 * w])
    u = _gelu(pb[:, 0:w])
    v = _layernorm(_gelu(pb[:, w:2 * w]), lng_ref[0], lnb_ref[0])
    vo_ref[0] = v
    y_b = u * (v * sguw_ref[0] + sgub_ref[0]) * _silu(pb[:, 2 * w:3 * w])

    pc = _dot(hb, win_ref[0, :, 7 * w:9 * w])
    c_x = pc[:, 0:w]
    gd = w // len(POOL_WINDOWS)
    yc_groups = []
    for g, win in enumerate(POOL_WINDOWS):
        s = c_x[:, g * gd:(g + 1) * gd]
        for j in range(POOL_BUF - (win - 1), POOL_BUF):
            s = s + spool_ref[0, :, j * w + g * gd:j * w + (g + 1) * gd]
        pm = s / float(min(win, PAST_LEN + 1)) - c_x[:, g * gd:(g + 1) * gd]
        yc_groups.append(_dot(pm.astype(BF16), poolw_ref[0, g]))
    y_c = jnp.concatenate(yc_groups, axis=1)
    y_c = (y_c + poolb_ref[0]) * pools_ref[0] * _silu(pc[:, w:2 * w])
    poolo_ref[0, :, 0:(POOL_BUF - 1) * w] = spool_ref[0, :, w:POOL_BUF * w]
    poolo_ref[0, :, (POOL_BUF - 1) * w:POOL_BUF * w] = c_x

    pg = _dot(hb, win_ref[0, :, 9 * w:9 * w + 3 * d])
    merged = (_sigmoid(pg[:, 0:d]) * _dot(y_a.astype(BF16), wbr_ref[0, 0])
              + _sigmoid(pg[:, d:2 * d]) * _dot(y_b.astype(BF16), wbr_ref[0, 1])
              + _sigmoid(pg[:, 2 * d:3 * d]) * _dot(y_c.astype(BF16), wbr_ref[0, 2]))
    x_new = x + gate * _dot(merged.astype(BF16), wout_ref[0])
    xs[...] = x_new

    @pl.when(l == pl.num_programs(0) - 1)
    def _():
        ms = jnp.mean(x_new * x_new, axis=-1, keepdims=True)
        y_ref[...] = x_new * lax.rsqrt(ms + EPS) * fg_ref[...]


def _sample_call(x, mod, sconv, spool, small, sgu_w00, sgu_b0, win, poolw, wbr, wout):
    rows, d = x.shape
    w = d // 2
    depth = win.shape[0]
    ng, convw, convb, lng, lnb, _, poolb, pools, fg = small
    per_layer = lambda a, **kw: pl.BlockSpec((1,) + a.shape[1:], lambda l: (l,) + (0,) * (a.ndim - 1), **kw)
    in_specs = [
        pl.BlockSpec((rows, d), lambda l: (0, 0)),
        per_layer(mod), per_layer(sconv), per_layer(spool), per_layer(ng),
        per_layer(win, pipeline_mode=pl.Buffered(1)),
        per_layer(convw), per_layer(convb), per_layer(lng), per_layer(lnb),
        per_layer(sgu_w00), per_layer(sgu_b0), per_layer(poolw), per_layer(poolb), per_layer(pools),
        per_layer(wbr), per_layer(wout),
        pl.BlockSpec(fg.shape, lambda l: (0, 0)),
    ]
    out_shape = [
        jax.ShapeDtypeStruct((rows, d), F32),
        jax.ShapeDtypeStruct(sconv.shape, F32),
        jax.ShapeDtypeStruct(spool.shape, F32),
        jax.ShapeDtypeStruct((depth, rows, w), F32),
    ]
    out_specs = [
        pl.BlockSpec((rows, d), lambda l: (0, 0)),
        per_layer(sconv), per_layer(spool),
        pl.BlockSpec((1, rows, w), lambda l: (l, 0, 0)),
    ]
    return pl.pallas_call(
        _sample_kernel,
        grid=(depth,),
        in_specs=in_specs,
        out_specs=out_specs,
        out_shape=out_shape,
        scratch_shapes=[pltpu.VMEM((rows, d), F32)],
        compiler_params=pltpu.CompilerParams(
            dimension_semantics=("arbitrary",),
            vmem_limit_bytes=VMEM_LIMIT_BYTES),
        name="sample_layers",
    )(x, mod, sconv, spool, ng, win, convw, convb, lng, lnb, sgu_w00, sgu_b0, poolw, poolb, pools,
      wbr, wout, fg)


def kernel(x_prompt, x_sample, c_prompt, c_sample, state_conv, state_pool, w_ada, b_ada, norm_g, w_in, conv_w,
           conv_b, lnv_g, lnv_b, sgu_w, sgu_b, pool_w, pool_b, pool_scale, w_branch, w_out, final_g):
    n, seq, d = x_prompt.shape
    rows = x_sample.shape[0]
    depth = w_in.shape[0]
    w = d // 2
    hd = w // SGU_HEADS
    assert x_sample.shape[1] == 1 and seq % SEQ_TILE == 0 and SEQ_TILE % CHUNK == 0

    pad = 2 * SUBLANES
    c_all = jnp.concatenate([jnp.pad(c_prompt, ((0, pad - n), (0, 0))), c_sample], axis=0)
    mod = _mod_call(c_all, w_ada, b_ada)
    mod_p = mod[:, :n].reshape(depth, n, 1, 3 * d)
    mod_s = mod[:, pad:pad + rows]

    row = lambda a: a.reshape(depth, 1, a.shape[-1])
    sgub_full = jnp.repeat(jnp.swapaxes(sgu_b, 1, 2), hd, axis=2)
    small = (row(norm_g), conv_w, row(conv_b), row(lnv_g), row(lnv_b), sgub_full, row(pool_b),
             row(pool_scale), final_g.reshape(1, d))
    win = w_in.astype(BF16)
    wbr = w_branch.astype(BF16)
    wout = w_out.astype(BF16)
    poolw = pool_w.astype(BF16)
    sguw = jnp.transpose(sgu_w, (0, 2, 1, 3)).reshape(depth, CHUNK, SGU_HEADS * CHUNK).astype(BF16)
    sgu_w00 = jnp.repeat(sgu_w[:, :, 0, 0], hd, axis=1).reshape(depth, 1, w)
    sgu_b0 = jnp.repeat(sgu_b[:, :, 0], hd, axis=1).reshape(depth, 1, w)

    xp = x_prompt
    conv_p, pool_p, v_p = [], [], []
    for l in range(depth):
        xp, cp, pp, vp = _prompt_call(l, l == depth - 1, xp, mod_p[l], small, win, sguw, poolw, wbr, wout)
        conv_p.append(cp); pool_p.append(pp); v_p.append(vp)

    y_s, conv_s, pool_s, v_s = _sample_call(
        x_sample.reshape(rows, d), mod_s,
        state_conv.reshape(depth, rows, (CONV_W - 1) * w), state_pool.reshape(depth, rows, POOL_BUF * w),
        small, sgu_w00, sgu_b0, win, poolw, wbr, wout)

    return (xp, y_s.reshape(rows, 1, d), jnp.stack(conv_p), conv_s.reshape(depth, rows, CONV_W - 1, w),
            jnp.stack(pool_p), pool_s.reshape(depth, rows, POOL_BUF, w), jnp.stack(v_p),
            v_s.reshape(depth, rows, 1, w))
```

```python
import functools
import math

import jax
import jax.numpy as jnp
from jax import lax
from jax.experimental import pallas as pl
from jax.experimental.pallas import tpu as pltpu

F32 = jnp.float32
BF16 = jnp.bfloat16

CONV_W = 3
CHUNK = 128
SGU_HEADS = 8
POOL_WINDOWS = (2, 4, 8, 16)
POOL_BUF = max(POOL_WINDOWS) - 1
PAST_LEN = 16384
EPS = 1e-6

SUBLANES = 8
SEQ_TILE = 512
HEADS_PER_DOT = 4
VMEM_LIMIT_BYTES = 52 * 1024 * 1024


def _sigmoid(x):
    return 0.5 * (1.0 + jnp.tanh(0.5 * x))


def _silu(x):
    return x * _sigmoid(x)


def _gelu(x):
    c = math.sqrt(2.0 / math.pi)
    return 0.5 * x * (1.0 + jnp.tanh(c * (x + 0.044715 * (x * x * x))))


def _shift_rows(x, k):
    return pltpu.roll(x, k, axis=0)


def _dot(a, b):
    return jnp.dot(a, b, preferred_element_type=F32)


def _modulated_norm(x, norm_g, shift, scale):
    ms = jnp.mean(x * x, axis=-1, keepdims=True)
    return (x * lax.rsqrt(ms + EPS)) * (norm_g * (1.0 + scale)) + shift


def _layernorm(x, g, b):
    mu = jnp.mean(x, axis=-1, keepdims=True)
    d = x - mu
    var = jnp.mean(d * d, axis=-1, keepdims=True)
    return d * lax.rsqrt(var + EPS) * g + b


def _mod_kernel(cp_ref, cs_ref, w_ref, b_ref, op_ref, os_ref):
    n = cp_ref.shape[0]
    c = jnp.concatenate([cp_ref[...], cp_ref[...], cs_ref[...]], axis=0)
    m = _dot(_silu(c).astype(BF16), w_ref[0].astype(BF16)) + b_ref[0]
    op_ref[0] = m[0:n]
    os_ref[0] = m[2 * n:]


def _mod_call(c_prompt, c_sample, w_ada, b_ada):
    depth, d, d3 = w_ada.shape
    n, rows = c_prompt.shape[0], c_sample.shape[0]
    assert n == SUBLANES
    col_block = d3 // 2
    return pl.pallas_call(
        _mod_kernel,
        grid=(depth, d3 // col_block),
        in_specs=[
            pl.BlockSpec((n, d), lambda l, j: (0, 0)),
            pl.BlockSpec((rows, d), lambda l, j: (0, 0)),
            pl.BlockSpec((1, d, col_block), lambda l, j: (l, 0, j)),
            pl.BlockSpec((1, 1, col_block), lambda l, j: (l, 0, j)),
        ],
        out_specs=[
            pl.BlockSpec((1, n, col_block), lambda l, j: (l, 0, j)),
            pl.BlockSpec((1, rows, col_block), lambda l, j: (l, 0, j)),
        ],
        out_shape=[
            jax.ShapeDtypeStruct((depth, n, d3), F32),
            jax.ShapeDtypeStruct((depth, rows, d3), F32),
        ],
        compiler_params=pltpu.CompilerParams(
            dimension_semantics=("arbitrary", "arbitrary"),
            vmem_limit_bytes=VMEM_LIMIT_BYTES),
        name="adaln_mod",
    )(c_prompt, c_sample, w_ada, b_ada.reshape(depth, 1, d3))


def _prompt_kernel(x_ref, mod_ref, ng_ref, win_ref, convw_ref, convb_ref, lng_ref, lnb_ref,
                   sguw_ref, sgub_ref, poolw_ref, poolb_ref, pools_ref, wbr_ref, wout_ref, fg_ref,
                   xo_ref, convo_ref, poolo_ref, vo_ref,
                   zbuf, cbuf, *, layer, final):
    t = pl.program_id(1)
    tm, d = x_ref.shape[1], x_ref.shape[2]
    w = d // 2
    zoff = SUBLANES
    coff = 2 * SUBLANES

    @pl.when(t == 0)
    def _():
        zbuf[0:zoff, :] = jnp.zeros((zoff, w), F32)
        cbuf[0:coff, :] = jnp.zeros((coff, w), F32)

    x = x_ref[0]
    mod = mod_ref[layer, pl.ds(pl.program_id(0), 1), :]
    shift, scale, gate = mod[:, 0:d], mod[:, d:2 * d], mod[:, 2 * d:3 * d]
    hb = _modulated_norm(x, ng_ref[layer], shift, scale).astype(BF16)

    pa = _dot(hb, win_ref[:, 0:4 * w])
    pb = _dot(hb, win_ref[:, 4 * w:7 * w])
    pc = _dot(hb, win_ref[:, 7 * w:9 * w])
    pg = _dot(hb, win_ref[:, 9 * w:9 * w + 3 * d])

    z = pa[:, w:2 * w] * pa[:, 2 * w:3 * w]
    zbuf[zoff:zoff + tm, :] = z
    zfull = zbuf[...]
    cw = convw_ref[layer]
    conv = convb_ref[layer] + (_shift_rows(zfull, 2)[zoff:, :] * cw[0:1]
                               + _shift_rows(zfull, 1)[zoff:, :] * cw[1:2]
                               + z * cw[2:3])
    y_a = pa[:, 0:w] * conv * _silu(pa[:, 3 * w:4 * w])
    zbuf[0:zoff, :] = zbuf[tm:tm + zoff, :]

    u = _gelu(pb[:, 0:w])
    v = _layernorm(_gelu(pb[:, w:2 * w]), lng_ref[layer], lnb_ref[layer])
    vb = v.astype(BF16)
    causal = (lax.broadcasted_iota(jnp.int32, (CHUNK, CHUNK), 1)
              <= lax.broadcasted_iota(jnp.int32, (CHUNK, CHUNK), 0))
    hd = w // SGU_HEADS
    gcols = HEADS_PER_DOT * hd
    lane_head = lax.broadcasted_iota(jnp.int32, (CHUNK, gcols), 1) // hd
    bias = sgub_ref[layer]
    wgs = [jnp.concatenate([jnp.where(causal, sguw_ref[g * HEADS_PER_DOT + hh], jnp.zeros((), BF16))
                            for hh in range(HEADS_PER_DOT)], axis=1)
           for g in range(SGU_HEADS // HEADS_PER_DOT)]
    mixed_rows = []
    for c in range(tm // CHUNK):
        vc = vb[c * CHUNK:(c + 1) * CHUNK, :]
        outs = []
        for g, wg in enumerate(wgs):
            vg = vc[:, g * gcols:(g + 1) * gcols]
            rhs = jnp.concatenate(
                [jnp.where(lane_head == hh, vg, jnp.zeros((), BF16)) for hh in range(HEADS_PER_DOT)], axis=0)
            outs.append(_dot(wg, rhs))
        mixed_rows.append(jnp.concatenate(outs, axis=1) + bias)
    mixed = jnp.concatenate(mixed_rows, axis=0)
    y_b = u * mixed * _silu(pb[:, 2 * w:3 * w])

    c_x = pc[:, 0:w]
    cbuf[coff:coff + tm, :] = c_x
    gd = w // len(POOL_WINDOWS)
    pos1 = lax.broadcasted_iota(jnp.int32, (tm, gd), 0) + (t * tm + 1)
    yc_groups = []
    for g, win in enumerate(POOL_WINDOWS):
        cols = slice(g * gd, (g + 1) * gd)
        s = cbuf[:, cols]
        k = 1
        while k < win:
            s = s + _shift_rows(s, k)
            k *= 2
        s = s[coff:, :]
        cnt = jnp.minimum(pos1, win).astype(F32)
        pm = s / cnt - c_x[:, cols]
        yc_groups.append(_dot(pm.astype(BF16), poolw_ref[g]))
    y_c = jnp.concatenate(yc_groups, axis=1)
    y_c = (y_c + poolb_ref[layer]) * pools_ref[layer] * _silu(pc[:, w:2 * w])

    convo_ref[0] = z[tm - (CONV_W - 1):tm, :]
    poolo_ref[0] = cbuf[coff + tm - POOL_BUF:coff + tm, :]
    vo_ref[0] = v[tm - CHUNK:tm, :]
    cbuf[0:coff, :] = cbuf[tm:tm + coff, :]

    merged = (_sigmoid(pg[:, 0:d]) * _dot(y_a.astype(BF16), wbr_ref[0])
              + _sigmoid(pg[:, d:2 * d]) * _dot(y_b.astype(BF16), wbr_ref[1])
              + _sigmoid(pg[:, 2 * d:3 * d]) * _dot(y_c.astype(BF16), wbr_ref[2]))
    x_new = x + gate * _dot(merged.astype(BF16), wout_ref[...])
    if final:
        ms = jnp.mean(x_new * x_new, axis=-1, keepdims=True)
        x_new = x_new * lax.rsqrt(ms + EPS) * fg_ref[...]
    xo_ref[0] = x_new


def _prompt_call(layer, final, x, mod, small, win, sguw, poolw, wbr, wout):
    n, seq, d = x.shape
    w = d // 2
    tm = SEQ_TILE
    in_cols = win.shape[-1]
    resident = dict(pipeline_mode=pl.Buffered(1))
    whole = lambda a: pl.BlockSpec(a.shape, lambda b, t: (0,) * a.ndim, **resident)
    ng, convw, convb, lng, lnb, sgub, poolb, pools, fg = small
    in_specs = [
        pl.BlockSpec((1, tm, d), lambda b, t: (b, t, 0)),
        whole(mod),
        whole(ng),
        pl.BlockSpec((None, d, in_cols), lambda b, t: (layer, 0, 0), **resident),
        whole(convw), whole(convb), whole(lng), whole(lnb),
        pl.BlockSpec((None,) + sguw.shape[1:], lambda b, t: (layer, 0, 0, 0), **resident),
        whole(sgub),
        pl.BlockSpec((None,) + poolw.shape[1:], lambda b, t: (layer, 0, 0, 0), **resident),
        whole(poolb), whole(pools),
        pl.BlockSpec((None,) + wbr.shape[1:], lambda b, t: (layer, 0, 0, 0), **resident),
        pl.BlockSpec((None, d, d), lambda b, t: (layer, 0, 0), **resident),
        whole(fg),
    ]
    out_specs = [
        pl.BlockSpec((1, tm, d), lambda b, t: (b, t, 0)),
        pl.BlockSpec((1, CONV_W - 1, w), lambda b, t: (b, 0, 0)),
        pl.BlockSpec((1, POOL_BUF, w), lambda b, t: (b, 0, 0)),
        pl.BlockSpec((1, CHUNK, w), lambda b, t: (b, 0, 0)),
    ]
    out_shape = [
        jax.ShapeDtypeStruct((n, seq, d), F32),
        jax.ShapeDtypeStruct((n, CONV_W - 1, w), F32),
        jax.ShapeDtypeStruct((n, POOL_BUF, w), F32),
        jax.ShapeDtypeStruct((n, CHUNK, w), F32),
    ]
    return pl.pallas_call(
        functools.partial(_prompt_kernel, layer=layer, final=final),
        grid=(n, seq // tm),
        in_specs=in_specs,
        out_specs=out_specs,
        out_shape=out_shape,
        scratch_shapes=[
            pltpu.VMEM((SUBLANES + tm, w), F32),
            pltpu.VMEM((2 * SUBLANES + tm, w), F32),
        ],
        compiler_params=pltpu.CompilerParams(
            dimension_semantics=("arbitrary", "arbitrary"),
            vmem_limit_bytes=VMEM_LIMIT_BYTES),
        name=f"prompt_layer{layer}",
    )(x, mod, ng, win, convw, convb, lng, lnb, sguw, sgub, poolw, poolb, pools, wbr, wout, fg)


def _sample_kernel(x_ref, mod_ref, sconv_ref, spool_ref, ng_ref, win_ref, convw_ref, convb_ref, lng_ref,
                   lnb_ref, sguw_ref, sgub_ref, poolw_ref, poolb_ref, pools_ref, wbr_ref, wout_ref, fg_ref,
                   y_ref, convo_ref, poolo_ref, vo_ref, xs):
    l = pl.program_id(0)
    d = x_ref.shape[1]
    w = d // 2

    @pl.when(l == 0)
    def _():
        xs[...] = x_ref[...]

    x = xs[...]
    mod = mod_ref[0]
    shift, scale, gate = mod[:, 0:d], mod[:, d:2 * d], mod[:, 2 * d:3 * d]
    hb = _modulated_norm(x, ng_ref[0], shift, scale).astype(BF16)

    pa = _dot(hb, win_ref[0, :, 0:4 * w])
    z = pa[:, w:2 * w] * pa[:, 2 * w:3 * w]
    prev = sconv_ref[0]
    cw = convw_ref[0]
    conv = convb_ref[0] + (prev[:, 0:w] * cw[0:1] + prev[:, w:2 * w] * cw[1:2] + z * cw[2:3])
    y_a = pa[:, 0:w] * conv * _silu(pa[:, 3 * w:4 * w])
    convo_ref[0, :, 0:w] = prev[:, w:2 * w]
    convo_ref[0, :, w:2 * w] = z

    pb = _dot(hb, win_ref[0, :, 4 * w:7 * w])
    u = _gelu(pb[:, 0:w])
    v = _layernorm(_gelu(pb[:, w:2 * w]), lng_ref[0], lnb_ref[0])
    vo_ref[0] = v
    y_b = u * (v * sguw_ref[0] + sgub_ref[0]) * _silu(pb[:, 2 * w:3 * w])

    pc = _dot(hb, win_ref[0, :, 7 * w:9 * w])
    c_x = pc[:, 0:w]
    gd = w // len(POOL_WINDOWS)
    yc_groups = []
    for g, win in enumerate(POOL_WINDOWS):
        s = c_x[:, g * gd:(g + 1) * gd]
        for j in range(POOL_BUF - (win - 1), POOL_BUF):
            s = s + spool_ref[0, :, j, g * gd:(g + 1) * gd]
        pm = s / float(min(win, PAST_LEN + 1)) - c_x[:, g * gd:(g + 1) * gd]
        yc_groups.append(_dot(pm.astype(BF16), poolw_ref[0, g]))
    y_c = jnp.concatenate(yc_groups, axis=1)
    y_c = (y_c + poolb_ref[0]) * pools_ref[0] * _silu(pc[:, w:2 * w])
    poolo_ref[0, :, 0:POOL_BUF - 1, :] = spool_ref[0, :, 1:POOL_BUF, :]
    poolo_ref[0, :, POOL_BUF - 1, :] = c_x

    pg = _dot(hb, win_ref[0, :, 9 * w:9 * w + 3 * d])
    merged = (_sigmoid(pg[:, 0:d]) * _dot(y_a.astype(BF16), wbr_ref[0, 0])
              + _sigmoid(pg[:, d:2 * d]) * _dot(y_b.astype(BF16), wbr_ref[0, 1])
              + _sigmoid(pg[:, 2 * d:3 * d]) * _dot(y_c.astype(BF16), wbr_ref[0, 2]))
    x_new = x + gate * _dot(merged.astype(BF16), wout_ref[0])
    xs[...] = x_new

    @pl.when(l == pl.num_programs(0) - 1)
    def _():
        ms = jnp.mean(x_new * x_new, axis=-1, keepdims=True)
        y_ref[...] = x_new * lax.rsqrt(ms + EPS) * fg_ref[...]


def _sample_call(x, mod, sconv, spool, small, sgu_w00, sgu_b0, win, poolw, wbr, wout):
    rows, d = x.shape
    w = d // 2
    depth = win.shape[0]
    ng, convw, convb, lng, lnb, _, poolb, pools, fg = small
    per_layer = lambda a, **kw: pl.BlockSpec((1,) + a.shape[1:], lambda l: (l,) + (0,) * (a.ndim - 1), **kw)
    in_specs = [
        pl.BlockSpec((rows, d), lambda l: (0, 0)),
        per_layer(mod), per_layer(sconv), per_layer(spool), per_layer(ng),
        per_layer(win, pipeline_mode=pl.Buffered(1)),
        per_layer(convw), per_layer(convb), per_layer(lng), per_layer(lnb),
        per_layer(sgu_w00), per_layer(sgu_b0), per_layer(poolw), per_layer(poolb), per_layer(pools),
        per_layer(wbr), per_layer(wout),
        pl.BlockSpec(fg.shape, lambda l: (0, 0)),
    ]
    out_shape = [
        jax.ShapeDtypeStruct((rows, d), F32),
        jax.ShapeDtypeStruct(sconv.shape, F32),
        jax.ShapeDtypeStruct(spool.shape, F32),
        jax.ShapeDtypeStruct((depth, rows, w), F32),
    ]
    out_specs = [
        pl.BlockSpec((rows, d), lambda l: (0, 0)),
        per_layer(sconv), per_layer(spool),
        pl.BlockSpec((1, rows, w), lambda l: (l, 0, 0)),
    ]
    return pl.pallas_call(
        _sample_kernel,
        grid=(depth,),
        in_specs=in_specs,
        out_specs=out_specs,
        out_shape=out_shape,
        scratch_shapes=[pltpu.VMEM((rows, d), F32)],
        compiler_params=pltpu.CompilerParams(
            dimension_semantics=("arbitrary",),
            vmem_limit_bytes=VMEM_LIMIT_BYTES),
        name="sample_layers",
    )(x, mod, sconv, spool, ng, win, convw, convb, lng, lnb, sgu_w00, sgu_b0, poolw, poolb, pools,
      wbr, wout, fg)


def kernel(x_prompt, x_sample, c_prompt, c_sample, state_conv, state_pool, w_ada, b_ada, norm_g, w_in, conv_w,
           conv_b, lnv_g, lnv_b, sgu_w, sgu_b, pool_w, pool_b, pool_scale, w_branch, w_out, final_g):
    n, seq, d = x_prompt.shape
    rows = x_sample.shape[0]
    depth = w_in.shape[0]
    w = d // 2
    hd = w // SGU_HEADS
    assert x_sample.shape[1] == 1 and seq % SEQ_TILE == 0 and SEQ_TILE % CHUNK == 0

    mod_p, mod_s = _mod_call(c_prompt, c_sample, w_ada, b_ada)

    row = lambda a: a.reshape(depth, 1, a.shape[-1])
    sgub_full = jnp.repeat(jnp.swapaxes(sgu_b, 1, 2), hd, axis=2)
    small = (row(norm_g), conv_w, row(conv_b), row(lnv_g), row(lnv_b), sgub_full, row(pool_b),
             row(pool_scale), final_g.reshape(1, d))
    win = w_in.astype(BF16)
    wbr = w_branch.astype(BF16)
    wout = w_out.astype(BF16)
    poolw = pool_w.astype(BF16)
    sguw = sgu_w.astype(BF16)
    sgu_w00 = jnp.repeat(sgu_w[:, :, 0, 0], hd, axis=1).reshape(depth, 1, w)
    sgu_b0 = jnp.repeat(sgu_b[:, :, 0], hd, axis=1).reshape(depth, 1, w)

    xp = x_prompt
    conv_p, pool_p, v_p = [], [], []
    for l in range(depth):
        xp, cp, pp, vp = _prompt_call(l, l == depth - 1, xp, mod_p, small, win, sguw, poolw, wbr, wout)
        conv_p.append(cp); pool_p.append(pp); v_p.append(vp)

    y_s, conv_s, pool_s, v_s = _sample_call(
        x_sample.reshape(rows, d), mod_s, state_conv.reshape(depth, rows, (CONV_W - 1) * w), state_pool,
        small, sgu_w00, sgu_b0, win, poolw, wbr, wout)

    return (xp, y_s.reshape(rows, 1, d), jnp.stack(conv_p), conv_s.reshape(depth, rows, CONV_W - 1, w),
            jnp.stack(pool_p), pool_s, jnp.stack(v_p), v_s.reshape(depth, rows, 1, w))
```

```python
import functools
import math

import jax
import jax.numpy as jnp
from jax import lax
from jax.experimental import pallas as pl
from jax.experimental.pallas import tpu as pltpu

F32 = jnp.float32
BF16 = jnp.bfloat16

CONV_W = 3
CHUNK = 128
SGU_HEADS = 8
POOL_WINDOWS = (2, 4, 8, 16)
POOL_BUF = max(POOL_WINDOWS) - 1
PAST_LEN = 16384
EPS = 1e-6

SUBLANES = 8
SEQ_TILE = 512
HEADS_PER_DOT = 4
VMEM_LIMIT_BYTES = 52 * 1024 * 1024


def _sigmoid(x):
    return 0.5 * (1.0 + jnp.tanh(0.5 * x))


def _silu(x):
    return x * _sigmoid(x)


def _gelu(x):
    c = math.sqrt(2.0 / math.pi)
    return 0.5 * x * (1.0 + jnp.tanh(c * (x + 0.044715 * (x * x * x))))


def _shift_rows(x, k):
    return pltpu.roll(x, k, axis=0)


def _dot(a, b):
    return jnp.dot(a, b, preferred_element_type=F32)


def _modulated_norm(x, norm_g, shift, scale):
    ms = jnp.mean(x * x, axis=-1, keepdims=True)
    return (x * lax.rsqrt(ms + EPS)) * (norm_g * (1.0 + scale)) + shift


def _layernorm(x, g, b):
    mu = jnp.mean(x, axis=-1, keepdims=True)
    d = x - mu
    var = jnp.mean(d * d, axis=-1, keepdims=True)
    return d * lax.rsqrt(var + EPS) * g + b


def _mod_kernel(cp_ref, cs_ref, w_ref, b_ref, op_ref, os_ref):
    n = cp_ref.shape[0]
    c = jnp.concatenate([cp_ref[...], cp_ref[...], cs_ref[...]], axis=0)
    m = _dot(_silu(c).astype(BF16), w_ref[0].astype(BF16)) + b_ref[0]
    op_ref[0] = m[0:n]
    os_ref[0] = m[2 * n:]


def _mod_call(c_prompt, c_sample, w_ada, b_ada):
    depth, d, d3 = w_ada.shape
    n, rows = c_prompt.shape[0], c_sample.shape[0]
    assert n == SUBLANES
    col_block = d3 // 2
    return pl.pallas_call(
        _mod_kernel,
        grid=(depth, d3 // col_block),
        in_specs=[
            pl.BlockSpec((n, d), lambda l, j: (0, 0)),
            pl.BlockSpec((rows, d), lambda l, j: (0, 0)),
            pl.BlockSpec((1, d, col_block), lambda l, j: (l, 0, j)),
            pl.BlockSpec((1, 1, col_block), lambda l, j: (l, 0, j)),
        ],
        out_specs=[
            pl.BlockSpec((1, n, col_block), lambda l, j: (l, 0, j)),
            pl.BlockSpec((1, rows, col_block), lambda l, j: (l, 0, j)),
        ],
        out_shape=[
            jax.ShapeDtypeStruct((depth, n, d3), F32),
            jax.ShapeDtypeStruct((depth, rows, d3), F32),
        ],
        compiler_params=pltpu.CompilerParams(
            dimension_semantics=("arbitrary", "arbitrary"),
            vmem_limit_bytes=VMEM_LIMIT_BYTES),
        name="adaln_mod",
    )(c_prompt, c_sample, w_ada, b_ada.reshape(depth, 1, d3))


def _prompt_kernel(x_ref, mod_ref, ng_ref, win_ref, convw_ref, convb_ref, lng_ref, lnb_ref,
                   sguw_ref, sgub_ref, poolw_ref, poolb_ref, pools_ref, wbr_ref, wout_ref, fg_ref,
                   xo_ref, convo_ref, poolo_ref, vo_ref,
                   zbuf, cbuf, *, layer, final):
    t = pl.program_id(1)
    tm, d = x_ref.shape[1], x_ref.shape[2]
    w = d // 2
    zoff = SUBLANES
    coff = 2 * SUBLANES

    @pl.when(t == 0)
    def _():
        zbuf[0:zoff, :] = jnp.zeros((zoff, w), F32)
        cbuf[0:coff, :] = jnp.zeros((coff, w), F32)

    x = x_ref[0]
    mod = mod_ref[layer, pl.ds(pl.program_id(0), 1), :]
    shift, scale, gate = mod[:, 0:d], mod[:, d:2 * d], mod[:, 2 * d:3 * d]
    hb = _modulated_norm(x, ng_ref[layer], shift, scale).astype(BF16)

    pa = _dot(hb, win_ref[:, 0:4 * w])
    pb = _dot(hb, win_ref[:, 4 * w:7 * w])
    pc = _dot(hb, win_ref[:, 7 * w:9 * w])
    pg = _dot(hb, win_ref[:, 9 * w:9 * w + 3 * d])

    z = pa[:, w:2 * w] * pa[:, 2 * w:3 * w]
    zbuf[zoff:zoff + tm, :] = z
    zfull = zbuf[...]
    cw = convw_ref[layer]
    conv = convb_ref[layer] + (_shift_rows(zfull, 2)[zoff:, :] * cw[0:1]
                               + _shift_rows(zfull, 1)[zoff:, :] * cw[1:2]
                               + z * cw[2:3])
    y_a = pa[:, 0:w] * conv * _silu(pa[:, 3 * w:4 * w])
    zbuf[0:zoff, :] = zbuf[tm:tm + zoff, :]

    u = _gelu(pb[:, 0:w])
    v = _layernorm(_gelu(pb[:, w:2 * w]), lng_ref[layer], lnb_ref[layer])
    vb = v.astype(BF16)
    causal = (lax.broadcasted_iota(jnp.int32, (CHUNK, CHUNK), 1)
              <= lax.broadcasted_iota(jnp.int32, (CHUNK, CHUNK), 0))
    hd = w // SGU_HEADS
    gcols = HEADS_PER_DOT * hd
    lane_head = lax.broadcasted_iota(jnp.int32, (CHUNK, gcols), 1) // hd
    bias = sgub_ref[layer]
    wgs = [jnp.concatenate([jnp.where(causal, sguw_ref[g * HEADS_PER_DOT + hh], jnp.zeros((), BF16))
                            for hh in range(HEADS_PER_DOT)], axis=1)
           for g in range(SGU_HEADS // HEADS_PER_DOT)]
    mixed_rows = []
    for c in range(tm // CHUNK):
        vc = vb[c * CHUNK:(c + 1) * CHUNK, :]
        outs = []
        for g, wg in enumerate(wgs):
            vg = vc[:, g * gcols:(g + 1) * gcols]
            rhs = jnp.concatenate(
                [jnp.where(lane_head == hh, vg, jnp.zeros((), BF16)) for hh in range(HEADS_PER_DOT)], axis=0)
            outs.append(_dot(wg, rhs))
        mixed_rows.append(jnp.concatenate(outs, axis=1) + bias)
    mixed = jnp.concatenate(mixed_rows, axis=0)
    y_b = u * mixed * _silu(pb[:, 2 * w:3 * w])

    c_x = pc[:, 0:w]
    cbuf[coff:coff + tm, :] = c_x
    gd = w // len(POOL_WINDOWS)
    pos1 = lax.broadcasted_iota(jnp.int32, (tm, gd), 0) + (t * tm + 1)
    yc_groups = []
    for g, win in enumerate(POOL_WINDOWS):
        cols = slice(g * gd, (g + 1) * gd)
        s = cbuf[:, cols]
        k = 1
        while k < win:
            s = s + _shift_rows(s, k)
            k *= 2
        s = s[coff:, :]
        cnt = jnp.minimum(pos1, win).astype(F32)
        pm = s / cnt - c_x[:, cols]
        yc_groups.append(_dot(pm.astype(BF16), poolw_ref[g]))
    y_c = jnp.concatenate(yc_groups, axis=1)
    y_c = (y_c + poolb_ref[layer]) * pools_ref[layer] * _silu(pc[:, w:2 * w])

    convo_ref[0] = z[tm - (CONV_W - 1):tm, :]
    for j in range(POOL_BUF):
        r = coff + tm - POOL_BUF + j
        poolo_ref[j, pl.ds(pl.program_id(0), 1), :] = cbuf[r:r + 1, :]
    vo_ref[0] = v[tm - CHUNK:tm, :]
    cbuf[0:coff, :] = cbuf[tm:tm + coff, :]

    merged = (_sigmoid(pg[:, 0:d]) * _dot(y_a.astype(BF16), wbr_ref[0])
              + _sigmoid(pg[:, d:2 * d]) * _dot(y_b.astype(BF16), wbr_ref[1])
              + _sigmoid(pg[:, 2 * d:3 * d]) * _dot(y_c.astype(BF16), wbr_ref[2]))
    x_new = x + gate * _dot(merged.astype(BF16), wout_ref[...])
    if final:
        ms = jnp.mean(x_new * x_new, axis=-1, keepdims=True)
        x_new = x_new * lax.rsqrt(ms + EPS) * fg_ref[...]
    xo_ref[0] = x_new


def _prompt_call(layer, final, x, mod, small, win, sguw, poolw, wbr, wout):
    n, seq, d = x.shape
    w = d // 2
    tm = SEQ_TILE
    in_cols = win.shape[-1]
    resident = dict(pipeline_mode=pl.Buffered(1))
    whole = lambda a: pl.BlockSpec(a.shape, lambda b, t: (0,) * a.ndim, **resident)
    ng, convw, convb, lng, lnb, sgub, poolb, pools, fg = small
    in_specs = [
        pl.BlockSpec((1, tm, d), lambda b, t: (b, t, 0)),
        whole(mod),
        whole(ng),
        pl.BlockSpec((None, d, in_cols), lambda b, t: (layer, 0, 0), **resident),
        whole(convw), whole(convb), whole(lng), whole(lnb),
        pl.BlockSpec((None,) + sguw.shape[1:], lambda b, t: (layer, 0, 0, 0), **resident),
        whole(sgub),
        pl.BlockSpec((None,) + poolw.shape[1:], lambda b, t: (layer, 0, 0, 0), **resident),
        whole(poolb), whole(pools),
        pl.BlockSpec((None,) + wbr.shape[1:], lambda b, t: (layer, 0, 0, 0), **resident),
        pl.BlockSpec((None, d, d), lambda b, t: (layer, 0, 0), **resident),
        whole(fg),
    ]
    out_specs = [
        pl.BlockSpec((1, tm, d), lambda b, t: (b, t, 0)),
        pl.BlockSpec((1, CONV_W - 1, w), lambda b, t: (b, 0, 0)),
        pl.BlockSpec((POOL_BUF, n, w), lambda b, t: (0, 0, 0)),
        pl.BlockSpec((1, CHUNK, w), lambda b, t: (b, 0, 0)),
    ]
    out_shape = [
        jax.ShapeDtypeStruct((n, seq, d), F32),
        jax.ShapeDtypeStruct((n, CONV_W - 1, w), F32),
        jax.ShapeDtypeStruct((POOL_BUF, n, w), F32),
        jax.ShapeDtypeStruct((n, CHUNK, w), F32),
    ]
    return pl.pallas_call(
        functools.partial(_prompt_kernel, layer=layer, final=final),
        grid=(n, seq // tm),
        in_specs=in_specs,
        out_specs=out_specs,
        out_shape=out_shape,
        scratch_shapes=[
            pltpu.VMEM((SUBLANES + tm, w), F32),
            pltpu.VMEM((2 * SUBLANES + tm, w), F32),
        ],
        compiler_params=pltpu.CompilerParams(
            dimension_semantics=("arbitrary", "arbitrary"),
            vmem_limit_bytes=VMEM_LIMIT_BYTES),
        name=f"prompt_layer{layer}",
    )(x, mod, ng, win, convw, convb, lng, lnb, sguw, sgub, poolw, poolb, pools, wbr, wout, fg)


def _sample_kernel(x_ref, mod_ref, sconv_ref, spool_ref, ng_ref, win_ref, convw_ref, convb_ref, lng_ref,
                   lnb_ref, sguw_ref, sgub_ref, poolw_ref, poolb_ref, pools_ref, wbr_ref, wout_ref, fg_ref,
                   y_ref, convo_ref, poolo_ref, vo_ref, xs):
    l = pl.program_id(0)
    d = x_ref.shape[1]
    w = d // 2

    @pl.when(l == 0)
    def _():
        xs[...] = x_ref[...]

    x = xs[...]
    mod = mod_ref[0]
    shift, scale, gate = mod[:, 0:d], mod[:, d:2 * d], mod[:, 2 * d:3 * d]
    hb = _modulated_norm(x, ng_ref[0], shift, scale).astype(BF16)

    pa = _dot(hb, win_ref[0, :, 0:4 * w])
    z = pa[:, w:2 * w] * pa[:, 2 * w:3 * w]
    prev = sconv_ref[0]
    cw = convw_ref[0]
    conv = convb_ref[0] + (prev[:, 0:w] * cw[0:1] + prev[:, w:2 * w] * cw[1:2] + z * cw[2:3])
    y_a = pa[:, 0:w] * conv * _silu(pa[:, 3 * w:4 * w])
    convo_ref[0, :, 0:w] = prev[:, w:2 * w]
    convo_ref[0, :, w:2 * w] = z

    pb = _dot(hb, win_ref[0, :, 4 * w:7 * w])
    u = _gelu(pb[:, 0:w])
    v = _layernorm(_gelu(pb[:, w:2 * w]), lng_ref[0], lnb_ref[0])
    vo_ref[0] = v
    y_b = u * (v * sguw_ref[0] + sgub_ref[0]) * _silu(pb[:, 2 * w:3 * w])

    pc = _dot(hb, win_ref[0, :, 7 * w:9 * w])
    c_x = pc[:, 0:w]
    gd = w // len(POOL_WINDOWS)
    yc_groups = []
    for g, win in enumerate(POOL_WINDOWS):
        s = c_x[:, g * gd:(g + 1) * gd]
        for j in range(POOL_BUF - (win - 1), POOL_BUF):
            s = s + spool_ref[0, j, :, g * gd:(g + 1) * gd]
        pm = s / float(min(win, PAST_LEN + 1)) - c_x[:, g * gd:(g + 1) * gd]
        yc_groups.append(_dot(pm.astype(BF16), poolw_ref[0, g]))
    y_c = jnp.concatenate(yc_groups, axis=1)
    y_c = (y_c + poolb_ref[0]) * pools_ref[0] * _silu(pc[:, w:2 * w])
    poolo_ref[0, 0:POOL_BUF - 1] = spool_ref[0, 1:POOL_BUF]
    poolo_ref[0, POOL_BUF - 1] = c_x

    pg = _dot(hb, win_ref[0, :, 9 * w:9 * w + 3 * d])
    merged = (_sigmoid(pg[:, 0:d]) * _dot(y_a.astype(BF16), wbr_ref[0, 0])
              + _sigmoid(pg[:, d:2 * d]) * _dot(y_b.astype(BF16), wbr_ref[0, 1])
              + _sigmoid(pg[:, 2 * d:3 * d]) * _dot(y_c.astype(BF16), wbr_ref[0, 2]))
    x_new = x + gate * _dot(merged.astype(BF16), wout_ref[0])
    xs[...] = x_new

    @pl.when(l == pl.num_programs(0) - 1)
    def _():
        ms = jnp.mean(x_new * x_new, axis=-1, keepdims=True)
        y_ref[...] = x_new * lax.rsqrt(ms + EPS) * fg_ref[...]


def _sample_call(x, mod, sconv, spool, small, sgu_w00, sgu_b0, win, poolw, wbr, wout):
    rows, d = x.shape
    w = d // 2
    depth = win.shape[0]
    ng, convw, convb, lng, lnb, _, poolb, pools, fg = small
    per_layer = lambda a, **kw: pl.BlockSpec((1,) + a.shape[1:], lambda l: (l,) + (0,) * (a.ndim - 1), **kw)
    in_specs = [
        pl.BlockSpec((rows, d), lambda l: (0, 0)),
        per_layer(mod), per_layer(sconv), per_layer(spool), per_layer(ng),
        per_layer(win, pipeline_mode=pl.Buffered(1)),
        per_layer(convw), per_layer(convb), per_layer(lng), per_layer(lnb),
        per_layer(sgu_w00), per_layer(sgu_b0), per_layer(poolw), per_layer(poolb), per_layer(pools),
        per_layer(wbr), per_layer(wout),
        pl.BlockSpec(fg.shape, lambda l: (0, 0)),
    ]
    out_shape = [
        jax.ShapeDtypeStruct((rows, d), F32),
        jax.ShapeDtypeStruct(sconv.shape, F32),
        jax.ShapeDtypeStruct(spool.shape, F32),
        jax.ShapeDtypeStruct((depth, rows, w), F32),
    ]
    out_specs = [
        pl.BlockSpec((rows, d), lambda l: (0, 0)),
        per_layer(sconv), per_layer(spool),
        pl.BlockSpec((1, rows, w), lambda l: (l, 0, 0)),
    ]
    return pl.pallas_call(
        _sample_kernel,
        grid=(depth,),
        in_specs=in_specs,
        out_specs=out_specs,
        out_shape=out_shape,
        scratch_shapes=[pltpu.VMEM((rows, d), F32)],
        compiler_params=pltpu.CompilerParams(
            dimension_semantics=("arbitrary",),
            vmem_limit_bytes=VMEM_LIMIT_BYTES),
        name="sample_layers",
    )(x, mod, sconv, spool, ng, win, convw, convb, lng, lnb, sgu_w00, sgu_b0, poolw, poolb, pools,
      wbr, wout, fg)


def kernel(x_prompt, x_sample, c_prompt, c_sample, state_conv, state_pool, w_ada, b_ada, norm_g, w_in, conv_w,
           conv_b, lnv_g, lnv_b, sgu_w, sgu_b, pool_w, pool_b, pool_scale, w_branch, w_out, final_g):
    n, seq, d = x_prompt.shape
    rows = x_sample.shape[0]
    depth = w_in.shape[0]
    w = d // 2
    hd = w // SGU_HEADS
    assert x_sample.shape[1] == 1 and seq % SEQ_TILE == 0 and SEQ_TILE % CHUNK == 0

    mod_p, mod_s = _mod_call(c_prompt, c_sample, w_ada, b_ada)

    row = lambda a: a.reshape(depth, 1, a.shape[-1])
    sgub_full = jnp.repeat(jnp.swapaxes(sgu_b, 1, 2), hd, axis=2)
    small = (row(norm_g), conv_w, row(conv_b), row(lnv_g), row(lnv_b), sgub_full, row(pool_b),
             row(pool_scale), final_g.reshape(1, d))
    win = w_in.astype(BF16)
    wbr = w_branch.astype(BF16)
    wout = w_out.astype(BF16)
    poolw = pool_w.astype(BF16)
    sguw = sgu_w.astype(BF16)
    sgu_w00 = jnp.repeat(sgu_w[:, :, 0, 0], hd, axis=1).reshape(depth, 1, w)
    sgu_b0 = jnp.repeat(sgu_b[:, :, 0], hd, axis=1).reshape(depth, 1, w)

    xp = x_prompt
    conv_p, pool_p, v_p = [], [], []
    for l in range(depth):
        xp, cp, pp, vp = _prompt_call(l, l == depth - 1, xp, mod_p, small, win, sguw, poolw, wbr, wout)
        conv_p.append(cp); pool_p.append(pp); v_p.append(vp)

    hist_major = lambda a: jnp.transpose(a, (0, 2, 1, 3))
    y_s, conv_s, pool_s, v_s = _sample_call(
        x_sample.reshape(rows, d), mod_s, state_conv.reshape(depth, rows, (CONV_W - 1) * w),
        hist_major(state_pool), small, sgu_w00, sgu_b0, win, poolw, wbr, wout)

    return (xp, y_s.reshape(rows, 1, d), jnp.stack(conv_p), conv_s.reshape(depth, rows, CONV_W - 1, w),
            hist_major(jnp.stack(pool_p)), hist_major(pool_s), jnp.stack(v_p), v_s.reshape(depth, rows, 1, w))
```

```python
import functools
import math

import jax
import jax.numpy as jnp
from jax import lax
from jax.experimental import pallas as pl
from jax.experimental.pallas import tpu as pltpu

F32 = jnp.float32
BF16 = jnp.bfloat16

CONV_W = 3
CHUNK = 128
SGU_HEADS = 8
POOL_WINDOWS = (2, 4, 8, 16)
POOL_BUF = max(POOL_WINDOWS) - 1
PAST_LEN = 16384
EPS = 1e-6

N_BRANCH = 3
SUBLANES = 8
BF16_SUBLANES = 16
SEQ_TILE = 512
HEADS_PER_DOT = 4
VMEM_LIMIT_BYTES = 52 * 1024 * 1024


def _sigmoid(x):
    return 0.5 * (1.0 + jnp.tanh(0.5 * x))


def _silu(x):
    return x * _sigmoid(x)


def _gelu(x):
    c = math.sqrt(2.0 / math.pi)
    return 0.5 * x * (1.0 + jnp.tanh(c * (x + 0.044715 * (x * x * x))))


def _shift_rows(x, k):
    return pltpu.roll(x, k, axis=0)


def _dot(a, b):
    return jnp.dot(a, b, preferred_element_type=F32)


def _modulated_norm(x, norm_g, shift, scale):
    ms = jnp.mean(x * x, axis=-1, keepdims=True)
    return (x * lax.rsqrt(ms + EPS)) * (norm_g * (1.0 + scale)) + shift


def _layernorm(x, g, b):
    mu = jnp.mean(x, axis=-1, keepdims=True)
    d = x - mu
    var = jnp.mean(d * d, axis=-1, keepdims=True)
    return d * lax.rsqrt(var + EPS) * g + b


def _mod_kernel(cp_ref, cs_ref, w_ref, b_ref, op_ref, os_ref):
    n = cp_ref.shape[0]
    c = jnp.concatenate([cp_ref[...], cp_ref[...], cs_ref[...]], axis=0)
    m = _dot(_silu(c).astype(BF16), w_ref[0].astype(BF16)) + b_ref[0]
    op_ref[0] = m[0:n]
    os_ref[0] = m[2 * n:]


def _mod_call(c_prompt, c_sample, w_ada, b_ada):
    depth, d, d3 = w_ada.shape
    n, rows = c_prompt.shape[0], c_sample.shape[0]
    assert n == SUBLANES
    col_block = d3 // 2
    return pl.pallas_call(
        _mod_kernel,
        grid=(depth, d3 // col_block),
        in_specs=[
            pl.BlockSpec((n, d), lambda l, j: (0, 0)),
            pl.BlockSpec((rows, d), lambda l, j: (0, 0)),
            pl.BlockSpec((1, d, col_block), lambda l, j: (l, 0, j)),
            pl.BlockSpec((1, 1, col_block), lambda l, j: (l, 0, j)),
        ],
        out_specs=[
            pl.BlockSpec((1, n, col_block), lambda l, j: (l, 0, j)),
            pl.BlockSpec((1, rows, col_block), lambda l, j: (l, 0, j)),
        ],
        out_shape=[
            jax.ShapeDtypeStruct((depth, n, d3), F32),
            jax.ShapeDtypeStruct((depth, rows, d3), F32),
        ],
        compiler_params=pltpu.CompilerParams(
            dimension_semantics=("arbitrary", "arbitrary"),
            vmem_limit_bytes=VMEM_LIMIT_BYTES),
        name="adaln_mod",
    )(c_prompt, c_sample, w_ada, b_ada.reshape(depth, 1, d3))


def _prompt_kernel(x_ref, mod_ref, ng_ref, win_ref, convw_ref, convb_ref, lng_ref, lnb_ref,
                   sguw_ref, sgub_ref, poolw_ref, poolb_ref, pools_ref, wbr_ref, wout_ref, fg_ref,
                   *rest, layer, final):
    if final:
        xo_ref, convo_ref, poolo_ref, vo_ref, zbuf, cbuf = rest
    else:
        (nwin_ref, nwbr_ref, nwout_ref, xo_ref, convo_ref, poolo_ref, vo_ref,
         nwin_o, nwbr_o, nwout_o, zbuf, cbuf) = rest
    t = pl.program_id(1)
    tm, d = x_ref.shape[1], x_ref.shape[2]
    w = d // 2
    zoff = SUBLANES
    coff = 2 * SUBLANES

    @pl.when(t == 0)
    def _():
        zbuf[0:zoff, :] = jnp.zeros((zoff, w), F32)
        cbuf[0:coff, :] = jnp.zeros((coff, w), F32)

    if not final:
        nwin_o[...] = nwin_ref[...].astype(BF16)
        nwbr_o[...] = nwbr_ref[...].astype(BF16)
        nwout_o[...] = nwout_ref[...].astype(BF16)

    x = x_ref[0]
    mod = mod_ref[layer, pl.ds(pl.program_id(0), 1), :]
    shift, scale, gate = mod[:, 0:d], mod[:, d:2 * d], mod[:, 2 * d:3 * d]
    hb = _modulated_norm(x, ng_ref[layer], shift, scale).astype(BF16)

    pa = _dot(hb, win_ref[:, 0:4 * w])
    pb = _dot(hb, win_ref[:, 4 * w:7 * w])
    pc = _dot(hb, win_ref[:, 7 * w:9 * w])
    pg = _dot(hb, win_ref[:, 9 * w:9 * w + 3 * d])

    z = pa[:, w:2 * w] * pa[:, 2 * w:3 * w]
    zbuf[zoff:zoff + tm, :] = z
    zfull = zbuf[...]
    cw = convw_ref[layer]
    conv = convb_ref[layer] + (_shift_rows(zfull, 2)[zoff:, :] * cw[0:1]
                               + _shift_rows(zfull, 1)[zoff:, :] * cw[1:2]
                               + z * cw[2:3])
    y_a = pa[:, 0:w] * conv * _silu(pa[:, 3 * w:4 * w])
    zbuf[0:zoff, :] = zbuf[tm:tm + zoff, :]

    u = _gelu(pb[:, 0:w])
    v = _layernorm(_gelu(pb[:, w:2 * w]), lng_ref[layer], lnb_ref[layer])
    vb = v.astype(BF16)
    causal = (lax.broadcasted_iota(jnp.int32, (CHUNK, CHUNK), 1)
              <= lax.broadcasted_iota(jnp.int32, (CHUNK, CHUNK), 0))
    hd = w // SGU_HEADS
    gcols = HEADS_PER_DOT * hd
    lane_head = lax.broadcasted_iota(jnp.int32, (CHUNK, gcols), 1) // hd
    bias = sgub_ref[layer]
    wgs = [jnp.concatenate([jnp.where(causal, sguw_ref[g * HEADS_PER_DOT + hh], jnp.zeros((), BF16))
                            for hh in range(HEADS_PER_DOT)], axis=1)
           for g in range(SGU_HEADS // HEADS_PER_DOT)]
    mixed_rows = []
    for c in range(tm // CHUNK):
        vc = vb[c * CHUNK:(c + 1) * CHUNK, :]
        outs = []
        for g, wg in enumerate(wgs):
            vg = vc[:, g * gcols:(g + 1) * gcols]
            rhs = jnp.concatenate(
                [jnp.where(lane_head == hh, vg, jnp.zeros((), BF16)) for hh in range(HEADS_PER_DOT)], axis=0)
            outs.append(_dot(wg, rhs))
        mixed_rows.append(jnp.concatenate(outs, axis=1) + bias)
    mixed = jnp.concatenate(mixed_rows, axis=0)
    y_b = u * mixed * _silu(pb[:, 2 * w:3 * w])

    c_x = pc[:, 0:w]
    cbuf[coff:coff + tm, :] = c_x
    gd = w // len(POOL_WINDOWS)
    pos1 = lax.broadcasted_iota(jnp.int32, (tm, gd), 0) + (t * tm + 1)
    yc_groups = []
    for g, win in enumerate(POOL_WINDOWS):
        cols = slice(g * gd, (g + 1) * gd)
        s = cbuf[:, cols]
        k = 1
        while k < win:
            s = s + _shift_rows(s, k)
            k *= 2
        s = s[coff:, :]
        cnt = jnp.minimum(pos1, win).astype(F32)
        pm = s / cnt - c_x[:, cols]
        yc_groups.append(_dot(pm.astype(BF16), poolw_ref[g]))
    y_c = jnp.concatenate(yc_groups, axis=1)
    y_c = (y_c + poolb_ref[layer]) * pools_ref[layer] * _silu(pc[:, w:2 * w])

    convo_ref[0] = z[tm - (CONV_W - 1):tm, :]
    for j in range(POOL_BUF):
        r = coff + tm - POOL_BUF + j
        poolo_ref[j, pl.ds(pl.program_id(0), 1), :] = cbuf[r:r + 1, :]
    vo_ref[0] = v[tm - CHUNK:tm, :]
    cbuf[0:coff, :] = cbuf[tm:tm + coff, :]

    merged = (_sigmoid(pg[:, 0:d]) * _dot(y_a.astype(BF16), wbr_ref[0:w, :])
              + _sigmoid(pg[:, d:2 * d]) * _dot(y_b.astype(BF16), wbr_ref[w:2 * w, :])
              + _sigmoid(pg[:, 2 * d:3 * d]) * _dot(y_c.astype(BF16), wbr_ref[2 * w:3 * w, :]))
    x_new = x + gate * _dot(merged.astype(BF16), wout_ref[...])
    if final:
        ms = jnp.mean(x_new * x_new, axis=-1, keepdims=True)
        x_new = x_new * lax.rsqrt(ms + EPS) * fg_ref[...]
    xo_ref[0] = x_new


def _prompt_call(layer, final, x, mod, small, weights, sguw, poolw, next_f32):
    n, seq, d = x.shape
    w = d // 2
    tm = SEQ_TILE
    n_tiles = seq // tm
    resident = dict(pipeline_mode=pl.Buffered(1))
    whole = lambda a: pl.BlockSpec(a.shape, lambda b, t: (0,) * a.ndim, **resident)
    ng, convw, convb, lng, lnb, sgub, poolb, pools, fg = small
    win, wbr, wout = weights
    in_specs = [
        pl.BlockSpec((1, tm, d), lambda b, t: (b, t, 0)),
        whole(mod),
        whole(ng),
        whole(win),
        whole(convw), whole(convb), whole(lng), whole(lnb),
        pl.BlockSpec((None,) + sguw.shape[1:], lambda b, t: (layer, 0, 0, 0), **resident),
        whole(sgub),
        pl.BlockSpec((None,) + poolw.shape[1:], lambda b, t: (layer, 0, 0, 0), **resident),
        whole(poolb), whole(pools),
        whole(wbr), whole(wout),
        whole(fg),
    ]
    out_specs = [
        pl.BlockSpec((1, tm, d), lambda b, t: (b, t, 0)),
        pl.BlockSpec((1, CONV_W - 1, w), lambda b, t: (b, 0, 0)),
        pl.BlockSpec((POOL_BUF, n, w), lambda b, t: (0, 0, 0)),
        pl.BlockSpec((1, CHUNK, w), lambda b, t: (b, 0, 0)),
    ]
    out_shape = [
        jax.ShapeDtypeStruct((n, seq, d), F32),
        jax.ShapeDtypeStruct((n, CONV_W - 1, w), F32),
        jax.ShapeDtypeStruct((POOL_BUF, n, w), F32),
        jax.ShapeDtypeStruct((n, CHUNK, w), F32),
    ]
    args = [x, mod, ng, win, convw, convb, lng, lnb, sguw, sgub, poolw, poolb, pools, wbr, wout, fg]
    if not final:
        steps = n * n_tiles
        for src in next_f32:
            rows, cols = src.shape[1:]
            assert rows % steps == 0 and (rows // steps) % BF16_SUBLANES == 0
            blk = rows // steps
            in_specs.append(pl.BlockSpec((None, blk, cols), lambda b, t: (layer + 1, b * n_tiles + t, 0)))
            out_specs.append(pl.BlockSpec((blk, cols), lambda b, t: (b * n_tiles + t, 0)))
            out_shape.append(jax.ShapeDtypeStruct((rows, cols), BF16))
            args.append(src)
    return pl.pallas_call(
        functools.partial(_prompt_kernel, layer=layer, final=final),
        grid=(n, n_tiles),
        in_specs=in_specs,
        out_specs=out_specs,
        out_shape=out_shape,
        scratch_shapes=[
            pltpu.VMEM((SUBLANES + tm, w), F32),
            pltpu.VMEM((2 * SUBLANES + tm, w), F32),
        ],
        compiler_params=pltpu.CompilerParams(
            dimension_semantics=("arbitrary", "arbitrary"),
            vmem_limit_bytes=VMEM_LIMIT_BYTES),
        name=f"prompt_layer{layer}",
    )(*args)


def _sample_kernel(x_ref, mod_ref, sconv_ref, spool_ref, ng_ref, win_ref, convw_ref, convb_ref, lng_ref,
                   lnb_ref, sguw_ref, sgub_ref, poolw_ref, poolb_ref, pools_ref, wbr_ref, wout_ref, fg_ref,
                   xo_ref, convo_ref, poolo_ref, vo_ref, *, final):
    d = x_ref.shape[1]
    w = d // 2
    x = x_ref[...]
    mod = mod_ref[...]
    shift, scale, gate = mod[:, 0:d], mod[:, d:2 * d], mod[:, 2 * d:3 * d]
    hb = _modulated_norm(x, ng_ref[...], shift, scale).astype(BF16)

    pa = _dot(hb, win_ref[:, 0:4 * w])
    z = pa[:, w:2 * w] * pa[:, 2 * w:3 * w]
    prev = sconv_ref[...]
    cw = convw_ref[...]
    conv = convb_ref[...] + (prev[:, 0:w] * cw[0:1] + prev[:, w:2 * w] * cw[1:2] + z * cw[2:3])
    y_a = pa[:, 0:w] * conv * _silu(pa[:, 3 * w:4 * w])
    convo_ref[:, 0:w] = prev[:, w:2 * w]
    convo_ref[:, w:2 * w] = z

    pb = _dot(hb, win_ref[:, 4 * w:7 * w])
    u = _gelu(pb[:, 0:w])
    v = _layernorm(_gelu(pb[:, w:2 * w]), lng_ref[...], lnb_ref[...])
    vo_ref[...] = v
    y_b = u * (v * sguw_ref[...] + sgub_ref[...]) * _silu(pb[:, 2 * w:3 * w])

    pc = _dot(hb, win_ref[:, 7 * w:9 * w])
    c_x = pc[:, 0:w]
    gd = w // len(POOL_WINDOWS)
    yc_groups = []
    for g, win in enumerate(POOL_WINDOWS):
        s = c_x[:, g * gd:(g + 1) * gd]
        for j in range(POOL_BUF - (win - 1), POOL_BUF):
            s = s + spool_ref[j, :, g * gd:(g + 1) * gd]
        pm = s / float(min(win, PAST_LEN + 1)) - c_x[:, g * gd:(g + 1) * gd]
        yc_groups.append(_dot(pm.astype(BF16), poolw_ref[g]))
    y_c = jnp.concatenate(yc_groups, axis=1)
    y_c = (y_c + poolb_ref[...]) * pools_ref[...] * _silu(pc[:, w:2 * w])
    poolo_ref[0:POOL_BUF - 1] = spool_ref[1:POOL_BUF]
    poolo_ref[POOL_BUF - 1] = c_x

    pg = _dot(hb, win_ref[:, 9 * w:9 * w + 3 * d])
    merged = (_sigmoid(pg[:, 0:d]) * _dot(y_a.astype(BF16), wbr_ref[0:w, :])
              + _sigmoid(pg[:, d:2 * d]) * _dot(y_b.astype(BF16), wbr_ref[w:2 * w, :])
              + _sigmoid(pg[:, 2 * d:3 * d]) * _dot(y_c.astype(BF16), wbr_ref[2 * w:3 * w, :]))
    x_new = x + gate * _dot(merged.astype(BF16), wout_ref[...])
    if final:
        ms = jnp.mean(x_new * x_new, axis=-1, keepdims=True)
        x_new = x_new * lax.rsqrt(ms + EPS) * fg_ref[...]
    xo_ref[...] = x_new


def _sample_call(layer, final, x, mod, sconv, spool, small, sgu_w00, sgu_b0, weights, poolw):
    rows, d = x.shape
    w = d // 2
    ng, convw, convb, lng, lnb, _, poolb, pools, fg = small
    win, wbr, wout = weights
    of_layer = lambda a: pl.BlockSpec((None,) + a.shape[1:], lambda i: (layer,) + (0,) * (a.ndim - 1))
    whole = lambda a: pl.BlockSpec(a.shape, lambda i: (0,) * a.ndim)
    in_specs = [
        whole(x), of_layer(mod), of_layer(sconv), of_layer(spool), of_layer(ng), whole(win),
        of_layer(convw), of_layer(convb), of_layer(lng), of_layer(lnb), of_layer(sgu_w00), of_layer(sgu_b0),
        of_layer(poolw), of_layer(poolb), of_layer(pools), whole(wbr), whole(wout), whole(fg),
    ]
    out_shape = [
        jax.ShapeDtypeStruct((rows, d), F32),
        jax.ShapeDtypeStruct(sconv.shape[1:], F32),
        jax.ShapeDtypeStruct(spool.shape[1:], F32),
        jax.ShapeDtypeStruct((rows, w), F32),
    ]
    return pl.pallas_call(
        functools.partial(_sample_kernel, final=final),
        grid=(1,),
        in_specs=in_specs,
        out_specs=[whole(o) for o in out_shape],
        out_shape=out_shape,
        compiler_params=pltpu.CompilerParams(
            dimension_semantics=("arbitrary",),
            vmem_limit_bytes=VMEM_LIMIT_BYTES),
        name=f"sample_layer{layer}",
    )(x, mod, sconv, spool, ng, win, convw, convb, lng, lnb, sgu_w00, sgu_b0, poolw, poolb, pools, wbr, wout, fg)


def kernel(x_prompt, x_sample, c_prompt, c_sample, state_conv, state_pool, w_ada, b_ada, norm_g, w_in, conv_w,
           conv_b, lnv_g, lnv_b, sgu_w, sgu_b, pool_w, pool_b, pool_scale, w_branch, w_out, final_g):
    n, seq, d = x_prompt.shape
    rows = x_sample.shape[0]
    depth = w_in.shape[0]
    w = d // 2
    hd = w // SGU_HEADS
    assert x_sample.shape[1] == 1 and seq % SEQ_TILE == 0 and SEQ_TILE % CHUNK == 0

    mod_p, mod_s = _mod_call(c_prompt, c_sample, w_ada, b_ada)

    row = lambda a: a.reshape(depth, 1, a.shape[-1])
    sgub_full = jnp.repeat(jnp.swapaxes(sgu_b, 1, 2), hd, axis=2)
    small = (row(norm_g), conv_w, row(conv_b), row(lnv_g), row(lnv_b), sgub_full, row(pool_b),
             row(pool_scale), final_g.reshape(1, d))
    poolw = pool_w.astype(BF16)
    sguw = sgu_w.astype(BF16)
    sgu_w00 = jnp.repeat(sgu_w[:, :, 0, 0], hd, axis=1).reshape(depth, 1, w)
    sgu_b0 = jnp.repeat(sgu_b[:, :, 0], hd, axis=1).reshape(depth, 1, w)
    hist_major = lambda a: jnp.transpose(a, (0, 2, 1, 3))
    sconv = state_conv.reshape(depth, rows, (CONV_W - 1) * w)
    spool = hist_major(state_pool)

    next_f32 = (w_in, w_branch.reshape(depth, N_BRANCH * w, d), w_out)
    weights = tuple(a[0].astype(BF16) for a in next_f32)
    xp, xs = x_prompt, x_sample.reshape(rows, d)
    conv_p, pool_p, v_p, conv_s, pool_s, v_s = [], [], [], [], [], []
    for l in range(depth):
        final = l == depth - 1
        xp, cp, pp, vp, *next_weights = _prompt_call(l, final, xp, mod_p, small, weights, sguw, poolw, next_f32)
        xs, cs, ps, vs = _sample_call(l, final, xs, mod_s, sconv, spool, small, sgu_w00, sgu_b0, weights, poolw)
        conv_p.append(cp); pool_p.append(pp); v_p.append(vp)
        conv_s.append(cs); pool_s.append(ps); v_s.append(vs)
        weights = tuple(next_weights)

    conv_s = jnp.stack(conv_s).reshape(depth, rows, CONV_W - 1, w)
    v_s = jnp.stack(v_s).reshape(depth, rows, 1, w)
    return (xp, xs.reshape(rows, 1, d), jnp.stack(conv_p), conv_s, hist_major(jnp.stack(pool_p)),
            hist_major(jnp.stack(pool_s)), jnp.stack(v_p), v_s)
```

```python
import functools
import math

import jax
import jax.numpy as jnp
from jax import lax
from jax.experimental import pallas as pl
from jax.experimental.pallas import tpu as pltpu

F32 = jnp.float32
BF16 = jnp.bfloat16

CONV_W = 3
CHUNK = 128
SGU_HEADS = 8
POOL_WINDOWS = (2, 4, 8, 16)
POOL_BUF = max(POOL_WINDOWS) - 1
PAST_LEN = 16384
EPS = 1e-6

N_BRANCH = 3
SUBLANES = 8
BF16_SUBLANES = 16
SEQ_TILE = 512
HEADS_PER_DOT = 4
VMEM_LIMIT_BYTES = 56 * 1024 * 1024


def _sigmoid(x):
    return 0.5 * (1.0 + jnp.tanh(0.5 * x))


def _silu(x):
    return x * _sigmoid(x)


def _gelu(x):
    c = math.sqrt(2.0 / math.pi)
    return 0.5 * x * (1.0 + jnp.tanh(c * (x + 0.044715 * (x * x * x))))


def _shift_rows(x, k):
    return pltpu.roll(x, k, axis=0)


def _dot(a, b):
    return jnp.dot(a, b, preferred_element_type=F32)


def _modulated_norm(x, norm_g, shift, scale):
    ms = jnp.mean(x * x, axis=-1, keepdims=True)
    return (x * lax.rsqrt(ms + EPS)) * (norm_g * (1.0 + scale)) + shift


def _layernorm(x, g, b):
    mu = jnp.mean(x, axis=-1, keepdims=True)
    d = x - mu
    var = jnp.mean(d * d, axis=-1, keepdims=True)
    return d * lax.rsqrt(var + EPS) * g + b


def _mod_kernel(cp_ref, cs_ref, w_ref, b_ref, op_ref, os_ref):
    n = cp_ref.shape[0]
    c = jnp.concatenate([cp_ref[...], cp_ref[...], cs_ref[...]], axis=0)
    m = _dot(_silu(c).astype(BF16), w_ref[0].astype(BF16)) + b_ref[0]
    op_ref[0] = m[0:n]
    os_ref[0] = m[2 * n:]


def _mod_call(c_prompt, c_sample, w_ada, b_ada):
    depth, d, d3 = w_ada.shape
    n, rows = c_prompt.shape[0], c_sample.shape[0]
    assert n == SUBLANES
    col_block = d3 // 2
    return pl.pallas_call(
        _mod_kernel,
        grid=(depth, d3 // col_block),
        in_specs=[
            pl.BlockSpec((n, d), lambda l, j: (0, 0)),
            pl.BlockSpec((rows, d), lambda l, j: (0, 0)),
            pl.BlockSpec((1, d, col_block), lambda l, j: (l, 0, j)),
            pl.BlockSpec((1, 1, col_block), lambda l, j: (l, 0, j)),
        ],
        out_specs=[
            pl.BlockSpec((1, n, col_block), lambda l, j: (l, 0, j)),
            pl.BlockSpec((1, rows, col_block), lambda l, j: (l, 0, j)),
        ],
        out_shape=[
            jax.ShapeDtypeStruct((depth, n, d3), F32),
            jax.ShapeDtypeStruct((depth, rows, d3), F32),
        ],
        compiler_params=pltpu.CompilerParams(
            dimension_semantics=("arbitrary", "arbitrary"),
            vmem_limit_bytes=VMEM_LIMIT_BYTES),
        name="adaln_mod",
    )(c_prompt, c_sample, w_ada, b_ada.reshape(depth, 1, d3))


def _prompt_tile(b, t, x_ref, mod_ref, ng_ref, win_ref, convw_ref, convb_ref, lng_ref, lnb_ref, sguw_ref,
                 sgub_ref, poolw_ref, poolb_ref, pools_ref, wbr_ref, wout_ref, fg_ref,
                 xo_ref, convo_ref, poolo_ref, vo_ref, zbuf, cbuf, *, layer, final):
    tm, d = x_ref.shape[1], x_ref.shape[2]
    w = d // 2
    zoff = SUBLANES
    coff = 2 * SUBLANES

    @pl.when(t == 0)
    def _():
        zbuf[0:zoff, :] = jnp.zeros((zoff, w), F32)
        cbuf[0:coff, :] = jnp.zeros((coff, w), F32)

    x = x_ref[0]
    mod = mod_ref[layer, pl.ds(b, 1), :]
    shift, scale, gate = mod[:, 0:d], mod[:, d:2 * d], mod[:, 2 * d:3 * d]
    hb = _modulated_norm(x, ng_ref[layer], shift, scale).astype(BF16)

    pa = _dot(hb, win_ref[:, 0:4 * w])
    pb = _dot(hb, win_ref[:, 4 * w:7 * w])
    pc = _dot(hb, win_ref[:, 7 * w:9 * w])
    pg = _dot(hb, win_ref[:, 9 * w:9 * w + 3 * d])

    z = pa[:, w:2 * w] * pa[:, 2 * w:3 * w]
    zbuf[zoff:zoff + tm, :] = z
    zfull = zbuf[...]
    cw = convw_ref[layer]
    conv = convb_ref[layer] + (_shift_rows(zfull, 2)[zoff:, :] * cw[0:1]
                               + _shift_rows(zfull, 1)[zoff:, :] * cw[1:2]
                               + z * cw[2:3])
    y_a = pa[:, 0:w] * conv * _silu(pa[:, 3 * w:4 * w])
    zbuf[0:zoff, :] = zbuf[tm:tm + zoff, :]

    u = _gelu(pb[:, 0:w])
    v = _layernorm(_gelu(pb[:, w:2 * w]), lng_ref[layer], lnb_ref[layer])
    vb = v.astype(BF16)
    causal = (lax.broadcasted_iota(jnp.int32, (CHUNK, CHUNK), 1)
              <= lax.broadcasted_iota(jnp.int32, (CHUNK, CHUNK), 0))
    hd = w // SGU_HEADS
    gcols = HEADS_PER_DOT * hd
    lane_head = lax.broadcasted_iota(jnp.int32, (CHUNK, gcols), 1) // hd
    bias = sgub_ref[layer]
    wgs = [jnp.concatenate([jnp.where(causal, sguw_ref[g * HEADS_PER_DOT + hh], jnp.zeros((), BF16))
                            for hh in range(HEADS_PER_DOT)], axis=1)
           for g in range(SGU_HEADS // HEADS_PER_DOT)]
    mixed_rows = []
    for c in range(tm // CHUNK):
        vc = vb[c * CHUNK:(c + 1) * CHUNK, :]
        outs = []
        for g, wg in enumerate(wgs):
            vg = vc[:, g * gcols:(g + 1) * gcols]
            rhs = jnp.concatenate(
                [jnp.where(lane_head == hh, vg, jnp.zeros((), BF16)) for hh in range(HEADS_PER_DOT)], axis=0)
            outs.append(_dot(wg, rhs))
        mixed_rows.append(jnp.concatenate(outs, axis=1) + bias)
    mixed = jnp.concatenate(mixed_rows, axis=0)
    y_b = u * mixed * _silu(pb[:, 2 * w:3 * w])

    c_x = pc[:, 0:w]
    cbuf[coff:coff + tm, :] = c_x
    gd = w // len(POOL_WINDOWS)
    pos1 = lax.broadcasted_iota(jnp.int32, (tm, gd), 0) + (t * tm + 1)
    yc_groups = []
    for g, win in enumerate(POOL_WINDOWS):
        cols = slice(g * gd, (g + 1) * gd)
        s = cbuf[:, cols]
        k = 1
        while k < win:
            s = s + _shift_rows(s, k)
            k *= 2
        s = s[coff:, :]
        cnt = jnp.minimum(pos1, win).astype(F32)
        pm = s / cnt - c_x[:, cols]
        yc_groups.append(_dot(pm.astype(BF16), poolw_ref[g]))
    y_c = jnp.concatenate(yc_groups, axis=1)
    y_c = (y_c + poolb_ref[layer]) * pools_ref[layer] * _silu(pc[:, w:2 * w])

    convo_ref[0] = z[tm - (CONV_W - 1):tm, :]
    for j in range(POOL_BUF):
        r = coff + tm - POOL_BUF + j
        poolo_ref[j, pl.ds(b, 1), :] = cbuf[r:r + 1, :]
    vo_ref[0] = v[tm - CHUNK:tm, :]
    cbuf[0:coff, :] = cbuf[tm:tm + coff, :]

    merged = (_sigmoid(pg[:, 0:d]) * _dot(y_a.astype(BF16), wbr_ref[0:w, :])
              + _sigmoid(pg[:, d:2 * d]) * _dot(y_b.astype(BF16), wbr_ref[w:2 * w, :])
              + _sigmoid(pg[:, 2 * d:3 * d]) * _dot(y_c.astype(BF16), wbr_ref[2 * w:3 * w, :]))
    x_new = x + gate * _dot(merged.astype(BF16), wout_ref[...])
    if final:
        ms = jnp.mean(x_new * x_new, axis=-1, keepdims=True)
        x_new = x_new * lax.rsqrt(ms + EPS) * fg_ref[...]
    xo_ref[0] = x_new


def _decode_rows(xs_ref, mods_ref, sconv_ref, spool_hbm, sguw0_ref, sgub0_ref, ng_ref, win_ref, convw_ref,
                 convb_ref, lng_ref, lnb_ref, poolw_ref, poolb_ref, pools_ref, wbr_ref, wout_ref, fg_ref,
                 xso_ref, convso_ref, vso_ref, poolso_hbm, hist, cx_stage, sem, *, layer, final):
    d = xs_ref.shape[1]
    w = d // 2
    gd = w // len(POOL_WINDOWS)

    keep = pltpu.make_async_copy(spool_hbm.at[layer, pl.ds(1, POOL_BUF - 1)],
                                 poolso_hbm.at[pl.ds(0, POOL_BUF - 1)], sem.at[0])
    keep.start()
    loads = [pltpu.make_async_copy(
        spool_hbm.at[layer, pl.ds(POOL_BUF - (win - 1), win - 1), :, pl.ds(g * gd, gd)], hist[g], sem.at[1 + g])
        for g, win in enumerate(POOL_WINDOWS)]
    for cp in loads:
        cp.start()

    x = xs_ref[...]
    mod = mods_ref[...]
    shift, scale, gate = mod[:, 0:d], mod[:, d:2 * d], mod[:, 2 * d:3 * d]
    hb = _modulated_norm(x, ng_ref[layer], shift, scale).astype(BF16)

    pa = _dot(hb, win_ref[:, 0:4 * w])
    z = pa[:, w:2 * w] * pa[:, 2 * w:3 * w]
    prev = sconv_ref[...]
    cw = convw_ref[layer]
    conv = convb_ref[layer] + (prev[:, 0:w] * cw[0:1] + prev[:, w:2 * w] * cw[1:2] + z * cw[2:3])
    y_a = pa[:, 0:w] * conv * _silu(pa[:, 3 * w:4 * w])
    convso_ref[:, 0:w] = prev[:, w:2 * w]
    convso_ref[:, w:2 * w] = z

    pb = _dot(hb, win_ref[:, 4 * w:7 * w])
    u = _gelu(pb[:, 0:w])
    v = _layernorm(_gelu(pb[:, w:2 * w]), lng_ref[layer], lnb_ref[layer])
    vso_ref[...] = v
    y_b = u * (v * sguw0_ref[layer] + sgub0_ref[layer]) * _silu(pb[:, 2 * w:3 * w])

    pc = _dot(hb, win_ref[:, 7 * w:9 * w])
    c_x = pc[:, 0:w]
    cx_stage[...] = c_x
    newest = pltpu.make_async_copy(cx_stage, poolso_hbm.at[POOL_BUF - 1], sem.at[1 + len(POOL_WINDOWS)])
    newest.start()
    yc_groups = []
    for g, win in enumerate(POOL_WINDOWS):
        loads[g].wait()
        s = c_x[:, g * gd:(g + 1) * gd]
        for j in range(win - 1):
            s = s + hist[g][j]
        pm = s / float(min(win, PAST_LEN + 1)) - c_x[:, g * gd:(g + 1) * gd]
        yc_groups.append(_dot(pm.astype(BF16), poolw_ref[g]))
    y_c = jnp.concatenate(yc_groups, axis=1)
    y_c = (y_c + poolb_ref[layer]) * pools_ref[layer] * _silu(pc[:, w:2 * w])

    pg = _dot(hb, win_ref[:, 9 * w:9 * w + 3 * d])
    merged = (_sigmoid(pg[:, 0:d]) * _dot(y_a.astype(BF16), wbr_ref[0:w, :])
              + _sigmoid(pg[:, d:2 * d]) * _dot(y_b.astype(BF16), wbr_ref[w:2 * w, :])
              + _sigmoid(pg[:, 2 * d:3 * d]) * _dot(y_c.astype(BF16), wbr_ref[2 * w:3 * w, :]))
    x_new = x + gate * _dot(merged.astype(BF16), wout_ref[...])
    if final:
        ms = jnp.mean(x_new * x_new, axis=-1, keepdims=True)
        x_new = x_new * lax.rsqrt(ms + EPS) * fg_ref[...]
    xso_ref[...] = x_new
    newest.wait()
    keep.wait()


def _layer_kernel(x_ref, mod_ref, ng_ref, win_ref, convw_ref, convb_ref, lng_ref, lnb_ref,
                  sguw_ref, sgub_ref, poolw_ref, poolb_ref, pools_ref, wbr_ref, wout_ref, fg_ref,
                  xs_ref, mods_ref, sconv_ref, spool_hbm, sguw0_ref, sgub0_ref,
                  *rest, layer, final, n_tiles, prompt_steps):
    n_pool = len(POOL_WINDOWS)
    if final:
        outs, scratch = rest[:8], rest[8:]
    else:
        casts_in, outs, casts_out, scratch = rest[:3], rest[3:11], rest[11:14], rest[14:]
    xo_ref, convo_ref, poolo_ref, vo_ref, xso_ref, convso_ref, vso_ref, poolso_hbm = outs
    zbuf, cbuf = scratch[:2]
    hist, (cx_stage, sem) = scratch[2:2 + n_pool], scratch[2 + n_pool:]
    step = pl.program_id(0)

    @pl.when(step < prompt_steps)
    def _():
        if not final:
            for src, dst in zip(casts_in, casts_out):
                dst[...] = src[...].astype(BF16)
        _prompt_tile(step // n_tiles, step % n_tiles, x_ref, mod_ref, ng_ref, win_ref, convw_ref, convb_ref,
                     lng_ref, lnb_ref, sguw_ref, sgub_ref, poolw_ref, poolb_ref, pools_ref, wbr_ref, wout_ref,
                     fg_ref, xo_ref, convo_ref, poolo_ref, vo_ref, zbuf, cbuf, layer=layer, final=final)

    @pl.when(step == prompt_steps)
    def _():
        _decode_rows(xs_ref, mods_ref, sconv_ref, spool_hbm, sguw0_ref, sgub0_ref, ng_ref, win_ref, convw_ref,
                     convb_ref, lng_ref, lnb_ref, poolw_ref, poolb_ref, pools_ref, wbr_ref, wout_ref, fg_ref,
                     xso_ref, convso_ref, vso_ref, poolso_hbm, hist, cx_stage, sem, layer=layer, final=final)


def _layer_call(layer, final, x, xs, mod_p, mod_s, sconv, spool, small, sgu_w00, sgu_b0, weights, sguw, poolw,
                next_f32):
    n, seq, d = x.shape
    rows = xs.shape[0]
    w = d // 2
    tm = SEQ_TILE
    n_tiles = seq // tm
    prompt_steps = n * n_tiles
    gd = w // len(POOL_WINDOWS)
    resident = dict(pipeline_mode=pl.Buffered(1))
    whole = lambda a: pl.BlockSpec(a.shape, lambda s: (0,) * a.ndim, **resident)
    of_layer = lambda a: pl.BlockSpec((None,) + a.shape[1:], lambda s: (layer,) + (0,) * (a.ndim - 1), **resident)
    ng, convw, convb, lng, lnb, sgub, poolb, pools, fg = small
    win, wbr, wout = weights

    ptile = lambda s: jnp.minimum(s, prompt_steps - 1)
    x_tile = lambda s: (ptile(s) // n_tiles, ptile(s) % n_tiles, 0)
    batch_row = lambda s: (ptile(s) // n_tiles, 0, 0)

    in_specs = [
        pl.BlockSpec((1, tm, d), x_tile),
        whole(mod_p), whole(ng), whole(win), whole(convw), whole(convb), whole(lng), whole(lnb),
        of_layer(sguw), whole(sgub), of_layer(poolw), whole(poolb), whole(pools), whole(wbr), whole(wout),
        whole(fg),
        whole(xs), of_layer(mod_s), of_layer(sconv), pl.BlockSpec(memory_space=pl.ANY),
        whole(sgu_w00), whole(sgu_b0),
    ]
    out_specs = [
        pl.BlockSpec((1, tm, d), x_tile),
        pl.BlockSpec((1, CONV_W - 1, w), batch_row),
        pl.BlockSpec((POOL_BUF, n, w), lambda s: (0, 0, 0)),
        pl.BlockSpec((1, CHUNK, w), batch_row),
        pl.BlockSpec((rows, d), lambda s: (0, 0)),
        pl.BlockSpec((rows, (CONV_W - 1) * w), lambda s: (0, 0)),
        pl.BlockSpec((rows, w), lambda s: (0, 0)),
        pl.BlockSpec(memory_space=pl.ANY),
    ]
    out_shape = [
        jax.ShapeDtypeStruct((n, seq, d), F32),
        jax.ShapeDtypeStruct((n, CONV_W - 1, w), F32),
        jax.ShapeDtypeStruct((POOL_BUF, n, w), F32),
        jax.ShapeDtypeStruct((n, CHUNK, w), F32),
        jax.ShapeDtypeStruct((rows, d), F32),
        jax.ShapeDtypeStruct((rows, (CONV_W - 1) * w), F32),
        jax.ShapeDtypeStruct((rows, w), F32),
        jax.ShapeDtypeStruct((POOL_BUF, rows, w), F32),
    ]
    args = [x, mod_p, ng, win, convw, convb, lng, lnb, sguw, sgub, poolw, poolb, pools, wbr, wout, fg,
            xs, mod_s, sconv, spool, sgu_w00, sgu_b0]
    if not final:
        for src in next_f32:
            nrows, cols = src.shape[1:]
            assert nrows % prompt_steps == 0 and (nrows // prompt_steps) % BF16_SUBLANES == 0
            blk = nrows // prompt_steps
            in_specs.append(pl.BlockSpec((None, blk, cols), lambda s: (layer + 1, ptile(s), 0)))
            out_specs.append(pl.BlockSpec((blk, cols), lambda s: (ptile(s), 0)))
            out_shape.append(jax.ShapeDtypeStruct((nrows, cols), BF16))
            args.append(src)
    scratch_shapes = [
        pltpu.VMEM((SUBLANES + tm, w), F32),
        pltpu.VMEM((2 * SUBLANES + tm, w), F32),
        *[pltpu.VMEM((win_len - 1, rows, gd), F32) for win_len in POOL_WINDOWS],
        pltpu.VMEM((rows, w), F32),
        pltpu.SemaphoreType.DMA((len(POOL_WINDOWS) + 2,)),
    ]
    return pl.pallas_call(
        functools.partial(_layer_kernel, layer=layer, final=final, n_tiles=n_tiles, prompt_steps=prompt_steps),
        grid=(prompt_steps + 1,),
        in_specs=in_specs,
        out_specs=out_specs,
        out_shape=out_shape,
        scratch_shapes=scratch_shapes,
        compiler_params=pltpu.CompilerParams(
            dimension_semantics=("arbitrary",),
            vmem_limit_bytes=VMEM_LIMIT_BYTES),
        name=f"layer{layer}",
    )(*args)


def kernel(x_prompt, x_sample, c_prompt, c_sample, state_conv, state_pool, w_ada, b_ada, norm_g, w_in, conv_w,
           conv_b, lnv_g, lnv_b, sgu_w, sgu_b, pool_w, pool_b, pool_scale, w_branch, w_out, final_g):
    n, seq, d = x_prompt.shape
    rows = x_sample.shape[0]
    depth = w_in.shape[0]
    w = d // 2
    hd = w // SGU_HEADS
    assert x_sample.shape[1] == 1 and seq % SEQ_TILE == 0 and SEQ_TILE % CHUNK == 0

    mod_p, mod_s = _mod_call(c_prompt, c_sample, w_ada, b_ada)

    row = lambda a: a.reshape(depth, 1, a.shape[-1])
    sgub_full = jnp.repeat(jnp.swapaxes(sgu_b, 1, 2), hd, axis=2)
    small = (row(norm_g), conv_w, row(conv_b), row(lnv_g), row(lnv_b), sgub_full, row(pool_b),
             row(pool_scale), final_g.reshape(1, d))
    poolw = pool_w.astype(BF16)
    sguw = sgu_w.astype(BF16)
    sgu_w00 = jnp.repeat(sgu_w[:, :, 0, 0], hd, axis=1).reshape(depth, 1, w)
    sgu_b0 = jnp.repeat(sgu_b[:, :, 0], hd, axis=1).reshape(depth, 1, w)
    hist_major = lambda a: jnp.transpose(a, (0, 2, 1, 3))
    sconv = state_conv.reshape(depth, rows, (CONV_W - 1) * w)
    spool = hist_major(state_pool)

    next_f32 = (w_in, w_branch.reshape(depth, N_BRANCH * w, d), w_out)
    weights = tuple(a[0].astype(BF16) for a in next_f32)
    xp, xs = x_prompt, x_sample.reshape(rows, d)
    conv_p, pool_p, v_p, conv_s, pool_s, v_s = [], [], [], [], [], []
    for l in range(depth):
        xp, cp, pp, vp, xs, cs, vs, ps, *next_weights = _layer_call(
            l, l == depth - 1, xp, xs, mod_p, mod_s, sconv, spool, small, sgu_w00, sgu_b0, weights, sguw, poolw,
            next_f32)
        conv_p.append(cp); pool_p.append(pp); v_p.append(vp)
        conv_s.append(cs); pool_s.append(ps); v_s.append(vs)
        weights = tuple(next_weights)

    conv_s = jnp.stack(conv_s).reshape(depth, rows, CONV_W - 1, w)
    v_s = jnp.stack(v_s).reshape(depth, rows, 1, w)
    return (xp, xs.reshape(rows, 1, d), jnp.stack(conv_p), conv_s, hist_major(jnp.stack(pool_p)),
            hist_major(jnp.stack(pool_s)), jnp.stack(v_p), v_s)
```

```python
import functools
import math

import jax
import jax.numpy as jnp
from jax import lax
from jax.experimental import pallas as pl
from jax.experimental.pallas import tpu as pltpu

F32 = jnp.float32
BF16 = jnp.bfloat16

CONV_W = 3
CHUNK = 128
SGU_HEADS = 8
POOL_WINDOWS = (2, 4, 8, 16)
POOL_BUF = max(POOL_WINDOWS) - 1
PAST_LEN = 16384
EPS = 1e-6

N_BRANCH = 3
SUBLANES = 8
BF16_SUBLANES = 16
SEQ_TILE = 512
HEADS_PER_DOT = 4
VMEM_LIMIT_BYTES = 58 * 1024 * 1024


def _sigmoid(x):
    return 0.5 * (1.0 + jnp.tanh(0.5 * x))


def _silu(x):
    return x * _sigmoid(x)


def _gelu(x):
    c = math.sqrt(2.0 / math.pi)
    return 0.5 * x * (1.0 + jnp.tanh(c * (x + 0.044715 * (x * x * x))))


def _shift_rows(x, k):
    return pltpu.roll(x, k, axis=0)


def _dot(a, b):
    return jnp.dot(a, b, preferred_element_type=F32)


def _modulated_norm(x, norm_g, shift, scale):
    ms = jnp.mean(x * x, axis=-1, keepdims=True)
    return (x * lax.rsqrt(ms + EPS)) * (norm_g * (1.0 + scale)) + shift


def _layernorm(x, g, b):
    mu = jnp.mean(x, axis=-1, keepdims=True)
    d = x - mu
    var = jnp.mean(d * d, axis=-1, keepdims=True)
    return d * lax.rsqrt(var + EPS) * g + b


def _mod_kernel(cp_ref, cs_ref, w_ref, b_ref, op_ref, os_ref):
    n = cp_ref.shape[0]
    c = jnp.concatenate([cp_ref[...], cp_ref[...], cs_ref[...]], axis=0)
    m = _dot(_silu(c).astype(BF16), w_ref[0].astype(BF16)) + b_ref[0]
    op_ref[0] = m[0:n]
    os_ref[0] = m[2 * n:]


def _mod_call(c_prompt, c_sample, w_ada, b_ada):
    depth, d, d3 = w_ada.shape
    n, rows = c_prompt.shape[0], c_sample.shape[0]
    assert n == SUBLANES
    col_block = d3 // 2
    return pl.pallas_call(
        _mod_kernel,
        grid=(depth, d3 // col_block),
        in_specs=[
            pl.BlockSpec((n, d), lambda l, j: (0, 0)),
            pl.BlockSpec((rows, d), lambda l, j: (0, 0)),
            pl.BlockSpec((1, d, col_block), lambda l, j: (l, 0, j)),
            pl.BlockSpec((1, 1, col_block), lambda l, j: (l, 0, j)),
        ],
        out_specs=[
            pl.BlockSpec((1, n, col_block), lambda l, j: (l, 0, j)),
            pl.BlockSpec((1, rows, col_block), lambda l, j: (l, 0, j)),
        ],
        out_shape=[
            jax.ShapeDtypeStruct((depth, n, d3), F32),
            jax.ShapeDtypeStruct((depth, rows, d3), F32),
        ],
        compiler_params=pltpu.CompilerParams(
            dimension_semantics=("arbitrary", "arbitrary"),
            vmem_limit_bytes=VMEM_LIMIT_BYTES),
        name="adaln_mod",
    )(c_prompt, c_sample, w_ada, b_ada.reshape(depth, 1, d3))


def _prompt_tile(b, t, x_ref, mod_ref, ng_ref, win_ref, convw_ref, convb_ref, lng_ref, lnb_ref, sguw_ref,
                 sgub_ref, poolw_ref, poolb_ref, pools_ref, wbr_ref, wout_ref, fg_ref,
                 xo_ref, convo_ref, poolo_ref, vo_ref, zbuf, cbuf, *, layer, final):
    tm, d = x_ref.shape[1], x_ref.shape[2]
    w = d // 2
    zoff = SUBLANES
    coff = 2 * SUBLANES

    @pl.when(t == 0)
    def _():
        zbuf[0:zoff, :] = jnp.zeros((zoff, w), F32)
        cbuf[0:coff, :] = jnp.zeros((coff, w), F32)

    x = x_ref[0]
    mod = mod_ref[layer, pl.ds(b, 1), :]
    shift, scale, gate = mod[:, 0:d], mod[:, d:2 * d], mod[:, 2 * d:3 * d]
    hb = _modulated_norm(x, ng_ref[layer], shift, scale).astype(BF16)

    pa = _dot(hb, win_ref[:, 0:4 * w])
    pb = _dot(hb, win_ref[:, 4 * w:7 * w])
    pc = _dot(hb, win_ref[:, 7 * w:9 * w])
    pg = _dot(hb, win_ref[:, 9 * w:9 * w + 3 * d])

    z = pa[:, w:2 * w] * pa[:, 2 * w:3 * w]
    zbuf[zoff:zoff + tm, :] = z
    zfull = zbuf[...]
    cw = convw_ref[layer]
    conv = convb_ref[layer] + (_shift_rows(zfull, 2)[zoff:, :] * cw[0:1]
                               + _shift_rows(zfull, 1)[zoff:, :] * cw[1:2]
                               + z * cw[2:3])
    y_a = pa[:, 0:w] * conv * _silu(pa[:, 3 * w:4 * w])
    zbuf[0:zoff, :] = zbuf[tm:tm + zoff, :]

    u = _gelu(pb[:, 0:w])
    v = _layernorm(_gelu(pb[:, w:2 * w]), lng_ref[layer], lnb_ref[layer])
    vb = v.astype(BF16)
    causal = (lax.broadcasted_iota(jnp.int32, (CHUNK, CHUNK), 1)
              <= lax.broadcasted_iota(jnp.int32, (CHUNK, CHUNK), 0))
    hd = w // SGU_HEADS
    gcols = HEADS_PER_DOT * hd
    lane_head = lax.broadcasted_iota(jnp.int32, (CHUNK, gcols), 1) // hd
    bias = sgub_ref[layer]
    wgs = [jnp.concatenate([jnp.where(causal, sguw_ref[g * HEADS_PER_DOT + hh], jnp.zeros((), BF16))
                            for hh in range(HEADS_PER_DOT)], axis=1)
           for g in range(SGU_HEADS // HEADS_PER_DOT)]
    mixed_rows = []
    for c in range(tm // CHUNK):
        vc = vb[c * CHUNK:(c + 1) * CHUNK, :]
        outs = []
        for g, wg in enumerate(wgs):
            vg = vc[:, g * gcols:(g + 1) * gcols]
            rhs = jnp.concatenate(
                [jnp.where(lane_head == hh, vg, jnp.zeros((), BF16)) for hh in range(HEADS_PER_DOT)], axis=0)
            outs.append(_dot(wg, rhs))
        mixed_rows.append(jnp.concatenate(outs, axis=1) + bias)
    mixed = jnp.concatenate(mixed_rows, axis=0)
    y_b = u * mixed * _silu(pb[:, 2 * w:3 * w])

    c_x = pc[:, 0:w]
    cbuf[coff:coff + tm, :] = c_x
    gd = w // len(POOL_WINDOWS)
    pos1 = lax.broadcasted_iota(jnp.int32, (tm, gd), 0) + (t * tm + 1)
    yc_groups = []
    for g, win in enumerate(POOL_WINDOWS):
        cols = slice(g * gd, (g + 1) * gd)
        s = cbuf[:, cols]
        k = 1
        while k < win:
            s = s + _shift_rows(s, k)
            k *= 2
        s = s[coff:, :]
        cnt = jnp.minimum(pos1, win).astype(F32)
        pm = s / cnt - c_x[:, cols]
        yc_groups.append(_dot(pm.astype(BF16), poolw_ref[g]))
    y_c = jnp.concatenate(yc_groups, axis=1)
    y_c = (y_c + poolb_ref[layer]) * pools_ref[layer] * _silu(pc[:, w:2 * w])

    convo_ref[0] = z[tm - (CONV_W - 1):tm, :]
    for j in range(POOL_BUF):
        r = coff + tm - POOL_BUF + j
        poolo_ref[j, pl.ds(b, 1), :] = cbuf[r:r + 1, :]
    vo_ref[0] = v[tm - CHUNK:tm, :]
    cbuf[0:coff, :] = cbuf[tm:tm + coff, :]

    merged = (_sigmoid(pg[:, 0:d]) * _dot(y_a.astype(BF16), wbr_ref[0:w, :])
              + _sigmoid(pg[:, d:2 * d]) * _dot(y_b.astype(BF16), wbr_ref[w:2 * w, :])
              + _sigmoid(pg[:, 2 * d:3 * d]) * _dot(y_c.astype(BF16), wbr_ref[2 * w:3 * w, :]))
    x_new = x + gate * _dot(merged.astype(BF16), wout_ref[...])
    if final:
        ms = jnp.mean(x_new * x_new, axis=-1, keepdims=True)
        x_new = x_new * lax.rsqrt(ms + EPS) * fg_ref[...]
    xo_ref[0] = x_new


def _decode_rows(xs_ref, mods_ref, sconv_ref, spool_hbm, sguw0_ref, sgub0_ref, ng_ref, win_ref, convw_ref,
                 convb_ref, lng_ref, lnb_ref, poolw_ref, poolb_ref, pools_ref, wbr_ref, wout_ref, fg_ref,
                 xso_ref, convso_ref, vso_ref, poolso_hbm, state, cx_stage, sem, *, layer, final):
    d = xs_ref.shape[1]
    w = d // 2
    gd = w // len(POOL_WINDOWS)

    load = pltpu.make_async_copy(spool_hbm.at[layer], state, sem.at[0])
    load.start()

    x = xs_ref[...]
    mod = mods_ref[...]
    shift, scale, gate = mod[:, 0:d], mod[:, d:2 * d], mod[:, 2 * d:3 * d]
    hb = _modulated_norm(x, ng_ref[layer], shift, scale).astype(BF16)

    pa = _dot(hb, win_ref[:, 0:4 * w])
    z = pa[:, w:2 * w] * pa[:, 2 * w:3 * w]
    prev = sconv_ref[...]
    cw = convw_ref[layer]
    conv = convb_ref[layer] + (prev[:, 0:w] * cw[0:1] + prev[:, w:2 * w] * cw[1:2] + z * cw[2:3])
    y_a = pa[:, 0:w] * conv * _silu(pa[:, 3 * w:4 * w])
    convso_ref[:, 0:w] = prev[:, w:2 * w]
    convso_ref[:, w:2 * w] = z

    pb = _dot(hb, win_ref[:, 4 * w:7 * w])
    u = _gelu(pb[:, 0:w])
    v = _layernorm(_gelu(pb[:, w:2 * w]), lng_ref[layer], lnb_ref[layer])
    vso_ref[...] = v
    y_b = u * (v * sguw0_ref[layer] + sgub0_ref[layer]) * _silu(pb[:, 2 * w:3 * w])

    pc = _dot(hb, win_ref[:, 7 * w:9 * w])
    c_x = pc[:, 0:w]
    cx_stage[...] = c_x
    newest = pltpu.make_async_copy(cx_stage, poolso_hbm.at[POOL_BUF - 1], sem.at[1])
    newest.start()
    load.wait()
    keep = pltpu.make_async_copy(state.at[pl.ds(1, POOL_BUF - 1)], poolso_hbm.at[pl.ds(0, POOL_BUF - 1)], sem.at[2])
    keep.start()
    yc_groups = []
    for g, win in enumerate(POOL_WINDOWS):
        s = c_x[:, g * gd:(g + 1) * gd]
        for j in range(POOL_BUF - (win - 1), POOL_BUF):
            s = s + state[j, :, g * gd:(g + 1) * gd]
        pm = s / float(min(win, PAST_LEN + 1)) - c_x[:, g * gd:(g + 1) * gd]
        yc_groups.append(_dot(pm.astype(BF16), poolw_ref[g]))
    y_c = jnp.concatenate(yc_groups, axis=1)
    y_c = (y_c + poolb_ref[layer]) * pools_ref[layer] * _silu(pc[:, w:2 * w])

    pg = _dot(hb, win_ref[:, 9 * w:9 * w + 3 * d])
    merged = (_sigmoid(pg[:, 0:d]) * _dot(y_a.astype(BF16), wbr_ref[0:w, :])
              + _sigmoid(pg[:, d:2 * d]) * _dot(y_b.astype(BF16), wbr_ref[w:2 * w, :])
              + _sigmoid(pg[:, 2 * d:3 * d]) * _dot(y_c.astype(BF16), wbr_ref[2 * w:3 * w, :]))
    x_new = x + gate * _dot(merged.astype(BF16), wout_ref[...])
    if final:
        ms = jnp.mean(x_new * x_new, axis=-1, keepdims=True)
        x_new = x_new * lax.rsqrt(ms + EPS) * fg_ref[...]
    xso_ref[...] = x_new
    newest.wait()
    keep.wait()


def _layer_kernel(x_ref, mod_ref, ng_ref, win_ref, convw_ref, convb_ref, lng_ref, lnb_ref,
                  sguw_ref, sgub_ref, poolw_ref, poolb_ref, pools_ref, wbr_ref, wout_ref, fg_ref,
                  xs_ref, mods_ref, sconv_ref, spool_hbm, sguw0_ref, sgub0_ref,
                  *rest, layer, final, n_tiles, prompt_steps):
    if final:
        outs, scratch = rest[:8], rest[8:]
    else:
        casts_in, outs, casts_out, scratch = rest[:3], rest[3:11], rest[11:14], rest[14:]
    xo_ref, convo_ref, poolo_ref, vo_ref, xso_ref, convso_ref, vso_ref, poolso_hbm = outs
    zbuf, cbuf, state, cx_stage, sem = scratch
    step = pl.program_id(0)

    @pl.when(step < prompt_steps)
    def _():
        if not final:
            for src, dst in zip(casts_in, casts_out):
                dst[...] = src[...].astype(BF16)
        _prompt_tile(step // n_tiles, step % n_tiles, x_ref, mod_ref, ng_ref, win_ref, convw_ref, convb_ref,
                     lng_ref, lnb_ref, sguw_ref, sgub_ref, poolw_ref, poolb_ref, pools_ref, wbr_ref, wout_ref,
                     fg_ref, xo_ref, convo_ref, poolo_ref, vo_ref, zbuf, cbuf, layer=layer, final=final)

    @pl.when(step == prompt_steps)
    def _():
        _decode_rows(xs_ref, mods_ref, sconv_ref, spool_hbm, sguw0_ref, sgub0_ref, ng_ref, win_ref, convw_ref,
                     convb_ref, lng_ref, lnb_ref, poolw_ref, poolb_ref, pools_ref, wbr_ref, wout_ref, fg_ref,
                     xso_ref, convso_ref, vso_ref, poolso_hbm, state, cx_stage, sem, layer=layer, final=final)


def _layer_call(layer, final, x, xs, mod_p, mod_s, sconv, spool, small, sgu_w00, sgu_b0, weights, sguw, poolw,
                next_f32):
    n, seq, d = x.shape
    rows = xs.shape[0]
    w = d // 2
    tm = SEQ_TILE
    n_tiles = seq // tm
    prompt_steps = n * n_tiles
    resident = dict(pipeline_mode=pl.Buffered(1))
    whole = lambda a: pl.BlockSpec(a.shape, lambda s: (0,) * a.ndim, **resident)
    of_layer = lambda a: pl.BlockSpec((None,) + a.shape[1:], lambda s: (layer,) + (0,) * (a.ndim - 1), **resident)
    ng, convw, convb, lng, lnb, sgub, poolb, pools, fg = small
    win, wbr, wout = weights

    ptile = lambda s: jnp.minimum(s, prompt_steps - 1)
    x_tile = lambda s: (ptile(s) // n_tiles, ptile(s) % n_tiles, 0)
    batch_row = lambda s: (ptile(s) // n_tiles, 0, 0)

    in_specs = [
        pl.BlockSpec((1, tm, d), x_tile),
        whole(mod_p), whole(ng), whole(win), whole(convw), whole(convb), whole(lng), whole(lnb),
        of_layer(sguw), whole(sgub), of_layer(poolw), whole(poolb), whole(pools), whole(wbr), whole(wout),
        whole(fg),
        whole(xs), of_layer(mod_s), of_layer(sconv), pl.BlockSpec(memory_space=pl.ANY),
        whole(sgu_w00), whole(sgu_b0),
    ]
    out_specs = [
        pl.BlockSpec((1, tm, d), x_tile),
        pl.BlockSpec((1, CONV_W - 1, w), batch_row),
        pl.BlockSpec((POOL_BUF, n, w), lambda s: (0, 0, 0)),
        pl.BlockSpec((1, CHUNK, w), batch_row),
        pl.BlockSpec((rows, d), lambda s: (0, 0)),
        pl.BlockSpec((rows, (CONV_W - 1) * w), lambda s: (0, 0)),
        pl.BlockSpec((rows, w), lambda s: (0, 0)),
        pl.BlockSpec(memory_space=pl.ANY),
    ]
    out_shape = [
        jax.ShapeDtypeStruct((n, seq, d), F32),
        jax.ShapeDtypeStruct((n, CONV_W - 1, w), F32),
        jax.ShapeDtypeStruct((POOL_BUF, n, w), F32),
        jax.ShapeDtypeStruct((n, CHUNK, w), F32),
        jax.ShapeDtypeStruct((rows, d), F32),
        jax.ShapeDtypeStruct((rows, (CONV_W - 1) * w), F32),
        jax.ShapeDtypeStruct((rows, w), F32),
        jax.ShapeDtypeStruct((POOL_BUF, rows, w), F32),
    ]
    args = [x, mod_p, ng, win, convw, convb, lng, lnb, sguw, sgub, poolw, poolb, pools, wbr, wout, fg,
            xs, mod_s, sconv, spool, sgu_w00, sgu_b0]
    if not final:
        for src in next_f32:
            nrows, cols = src.shape[1:]
            assert nrows % prompt_steps == 0 and (nrows // prompt_steps) % BF16_SUBLANES == 0
            blk = nrows // prompt_steps
            in_specs.append(pl.BlockSpec((None, blk, cols), lambda s: (layer + 1, ptile(s), 0)))
            out_specs.append(pl.BlockSpec((blk, cols), lambda s: (ptile(s), 0)))
            out_shape.append(jax.ShapeDtypeStruct((nrows, cols), BF16))
            args.append(src)
    scratch_shapes = [
        pltpu.VMEM((SUBLANES + tm, w), F32),
        pltpu.VMEM((2 * SUBLANES + tm, w), F32),
        pltpu.VMEM((POOL_BUF, rows, w), F32),
        pltpu.VMEM((rows, w), F32),
        pltpu.SemaphoreType.DMA((3,)),
    ]
    return pl.pallas_call(
        functools.partial(_layer_kernel, layer=layer, final=final, n_tiles=n_tiles, prompt_steps=prompt_steps),
        grid=(prompt_steps + 1,),
        in_specs=in_specs,
        out_specs=out_specs,
        out_shape=out_shape,
        scratch_shapes=scratch_shapes,
        compiler_params=pltpu.CompilerParams(
            dimension_semantics=("arbitrary",),
            vmem_limit_bytes=VMEM_LIMIT_BYTES),
        name=f"layer{layer}",
    )(*args)


def kernel(x_prompt, x_sample, c_prompt, c_sample, state_conv, state_pool, w_ada, b_ada, norm_g, w_in, conv_w,
           conv_b, lnv_g, lnv_b, sgu_w, sgu_b, pool_w, pool_b, pool_scale, w_branch, w_out, final_g):
    n, seq, d = x_prompt.shape
    rows = x_sample.shape[0]
    depth = w_in.shape[0]
    w = d // 2
    hd = w // SGU_HEADS
    assert x_sample.shape[1] == 1 and seq % SEQ_TILE == 0 and SEQ_TILE % CHUNK == 0

    mod_p, mod_s = _mod_call(c_prompt, c_sample, w_ada, b_ada)

    row = lambda a: a.reshape(depth, 1, a.shape[-1])
    sgub_full = jnp.repeat(jnp.swapaxes(sgu_b, 1, 2), hd, axis=2)
    small = (row(norm_g), conv_w, row(conv_b), row(lnv_g), row(lnv_b), sgub_full, row(pool_b),
             row(pool_scale), final_g.reshape(1, d))
    poolw = pool_w.astype(BF16)
    sguw = sgu_w.astype(BF16)
    sgu_w00 = jnp.repeat(sgu_w[:, :, 0, 0], hd, axis=1).reshape(depth, 1, w)
    sgu_b0 = jnp.repeat(sgu_b[:, :, 0], hd, axis=1).reshape(depth, 1, w)
    hist_major = lambda a: jnp.transpose(a, (0, 2, 1, 3))
    sconv = state_conv.reshape(depth, rows, (CONV_W - 1) * w)
    spool = hist_major(state_pool)

    next_f32 = (w_in, w_branch.reshape(depth, N_BRANCH * w, d), w_out)
    weights = tuple(a[0].astype(BF16) for a in next_f32)
    xp, xs = x_prompt, x_sample.reshape(rows, d)
    conv_p, pool_p, v_p, conv_s, pool_s, v_s = [], [], [], [], [], []
    for l in range(depth):
        xp, cp, pp, vp, xs, cs, vs, ps, *next_weights = _layer_call(
            l, l == depth - 1, xp, xs, mod_p, mod_s, sconv, spool, small, sgu_w00, sgu_b0, weights, sguw, poolw,
            next_f32)
        conv_p.append(cp); pool_p.append(pp); v_p.append(vp)
        conv_s.append(cs); pool_s.append(ps); v_s.append(vs)
        weights = tuple(next_weights)

    conv_s = jnp.stack(conv_s).reshape(depth, rows, CONV_W - 1, w)
    v_s = jnp.stack(v_s).reshape(depth, rows, 1, w)
    return (xp, xs.reshape(rows, 1, d), jnp.stack(conv_p), conv_s, hist_major(jnp.stack(pool_p)),
            hist_major(jnp.stack(pool_s)), jnp.stack(v_p), v_s)
```

```python
import functools
import math

import jax
import jax.numpy as jnp
from jax import lax
from jax.experimental import pallas as pl
from jax.experimental.pallas import tpu as pltpu

F32 = jnp.float32
BF16 = jnp.bfloat16

CONV_W = 3
CHUNK = 128
SGU_HEADS = 8
POOL_WINDOWS = (2, 4, 8, 16)
POOL_BUF = max(POOL_WINDOWS) - 1
PAST_LEN = 16384
EPS = 1e-6

N_BRANCH = 3
SUBLANES = 8
BF16_SUBLANES = 16
SEQ_TILE = 512
HEADS_PER_DOT = 2
VMEM_LIMIT_BYTES = 58 * 1024 * 1024


def _sigmoid(x):
    return 0.5 * (1.0 + jnp.tanh(0.5 * x))


def _silu(x):
    return x * _sigmoid(x)


def _gelu(x):
    c = math.sqrt(2.0 / math.pi)
    return 0.5 * x * (1.0 + jnp.tanh(c * (x + 0.044715 * (x * x * x))))


def _shift_rows(x, k):
    return pltpu.roll(x, k, axis=0)


def _dot(a, b):
    return jnp.dot(a, b, preferred_element_type=F32)


def _modulated_norm(x, norm_g, shift, scale):
    ms = jnp.mean(x * x, axis=-1, keepdims=True)
    return (x * lax.rsqrt(ms + EPS)) * (norm_g * (1.0 + scale)) + shift


_ROW_FIELDS = (("norm_g", 2), ("conv_b", 1), ("lnv_g", 1), ("lnv_b", 1), ("pool_b", 1), ("pool_scale", 1),
               ("sgu_w0", 1), ("sgu_b0", 1), ("conv_w0", 1), ("conv_w1", 1), ("conv_w2", 1))


def _layer_rows(rows_ref, layer, w):
    out, lo = {}, 0
    for name, units in _ROW_FIELDS:
        out[name] = rows_ref[layer, :, lo:lo + units * w]
        lo += units * w
    return out


def _layernorm(x, g, b):
    mu = jnp.mean(x, axis=-1, keepdims=True)
    d = x - mu
    var = jnp.mean(d * d, axis=-1, keepdims=True)
    return d * lax.rsqrt(var + EPS) * g + b


def _mod_kernel(cp_ref, cs_ref, w_ref, b_ref, *rest):
    n_cast = len(rest) // 2 - 1
    casts_in, (op_ref, os_ref), casts_out = rest[:n_cast], rest[n_cast:n_cast + 2], rest[n_cast + 2:]
    n = cp_ref.shape[0]
    c = jnp.concatenate([cp_ref[...], cp_ref[...], cs_ref[...]], axis=0)
    m = _dot(_silu(c).astype(BF16), w_ref[0].astype(BF16)) + b_ref[0]
    op_ref[0] = m[0:n]
    os_ref[0] = m[2 * n:]
    for src, dst in zip(casts_in, casts_out):
        dst[...] = src[...].astype(BF16)


def _mod_call(c_prompt, c_sample, w_ada, b_ada, first_f32):
    depth, d, d3 = w_ada.shape
    n, rows = c_prompt.shape[0], c_sample.shape[0]
    assert n == SUBLANES
    n_col = 2
    col_block = d3 // n_col
    steps = depth * n_col
    in_specs = [
        pl.BlockSpec((n, d), lambda l, j: (0, 0)),
        pl.BlockSpec((rows, d), lambda l, j: (0, 0)),
        pl.BlockSpec((1, d, col_block), lambda l, j: (l, 0, j)),
        pl.BlockSpec((1, 1, col_block), lambda l, j: (l, 0, j)),
    ]
    out_specs = [
        pl.BlockSpec((1, n, col_block), lambda l, j: (l, 0, j)),
        pl.BlockSpec((1, rows, col_block), lambda l, j: (l, 0, j)),
    ]
    out_shape = [
        jax.ShapeDtypeStruct((depth, n, d3), F32),
        jax.ShapeDtypeStruct((depth, rows, d3), F32),
    ]
    for src in first_f32:
        nrows, cols = src.shape[1:]
        assert nrows % steps == 0 and (nrows // steps) % BF16_SUBLANES == 0
        blk = nrows // steps
        in_specs.append(pl.BlockSpec((None, blk, cols), lambda l, j: (0, l * n_col + j, 0)))
        out_specs.append(pl.BlockSpec((blk, cols), lambda l, j: (l * n_col + j, 0)))
        out_shape.append(jax.ShapeDtypeStruct((nrows, cols), BF16))
    return pl.pallas_call(
        _mod_kernel,
        grid=(depth, n_col),
        in_specs=in_specs,
        out_specs=out_specs,
        out_shape=out_shape,
        compiler_params=pltpu.CompilerParams(
            dimension_semantics=("arbitrary", "arbitrary"),
            vmem_limit_bytes=VMEM_LIMIT_BYTES),
        name="adaln_mod",
    )(c_prompt, c_sample, w_ada, b_ada.reshape(depth, 1, d3), *first_f32)


def _prompt_tile(b, t, x_ref, mod_ref, rows_ref, win_ref, sguw_ref, sgub_ref, poolw_ref, wbr_ref, wout_ref, fg_ref,
                 xo_ref, convo_ref, poolo_ref, vo_ref, zbuf, cbuf, *, layer, final):
    tm, d = x_ref.shape[1], x_ref.shape[2]
    w = d // 2
    p = _layer_rows(rows_ref, layer, w)
    zoff = SUBLANES
    coff = 2 * SUBLANES

    @pl.when(t == 0)
    def _():
        zbuf[0:zoff, :] = jnp.zeros((zoff, w), F32)
        cbuf[0:coff, :] = jnp.zeros((coff, w), F32)

    x = x_ref[0]
    mod = mod_ref[layer, pl.ds(b, 1), :]
    shift, scale, gate = mod[:, 0:d], mod[:, d:2 * d], mod[:, 2 * d:3 * d]
    hb = _modulated_norm(x, p["norm_g"], shift, scale).astype(BF16)

    pc = _dot(hb, win_ref[:, 7 * w:9 * w])
    pb = _dot(hb, win_ref[:, 4 * w:7 * w])
    pa = _dot(hb, win_ref[:, 0:4 * w])
    pg = _dot(hb, win_ref[:, 9 * w:9 * w + 3 * d])

    c_x = pc[:, 0:w]
    cbuf[coff:coff + tm, :] = c_x
    gd = w // len(POOL_WINDOWS)
    pos1 = lax.broadcasted_iota(jnp.int32, (tm, gd), 0) + (t * tm + 1)
    yc_groups = []
    for g, win in enumerate(POOL_WINDOWS):
        cols = slice(g * gd, (g + 1) * gd)
        s = cbuf[:, cols]
        k = 1
        while k < win:
            s = s + _shift_rows(s, k)
            k *= 2
        s = s[coff:, :]
        cnt = jnp.minimum(pos1, win).astype(F32)
        pm = s / cnt - c_x[:, cols]
        yc_groups.append(_dot(pm.astype(BF16), poolw_ref[g].astype(BF16)))
    y_c = jnp.concatenate(yc_groups, axis=1)
    y_c = (y_c + p["pool_b"]) * p["pool_scale"] * _silu(pc[:, w:2 * w])

    u = _gelu(pb[:, 0:w])
    v = _layernorm(_gelu(pb[:, w:2 * w]), p["lnv_g"], p["lnv_b"])
    vb = v.astype(BF16)
    causal = (lax.broadcasted_iota(jnp.int32, (CHUNK, CHUNK), 1)
              <= lax.broadcasted_iota(jnp.int32, (CHUNK, CHUNK), 0))
    hd = w // SGU_HEADS
    gcols = HEADS_PER_DOT * hd
    lane_head = lax.broadcasted_iota(jnp.int32, (CHUNK, gcols), 1) // hd
    bias = sgub_ref[layer]
    wgs = [jnp.concatenate([jnp.where(causal, sguw_ref[g * HEADS_PER_DOT + hh], 0.0).astype(BF16)
                            for hh in range(HEADS_PER_DOT)], axis=1)
           for g in range(SGU_HEADS // HEADS_PER_DOT)]
    mixed_rows = []
    for c in range(tm // CHUNK):
        vc = vb[c * CHUNK:(c + 1) * CHUNK, :]
        outs = []
        for g, wg in enumerate(wgs):
            vg = vc[:, g * gcols:(g + 1) * gcols]
            rhs = jnp.concatenate(
                [jnp.where(lane_head == hh, vg, jnp.zeros((), BF16)) for hh in range(HEADS_PER_DOT)], axis=0)
            outs.append(_dot(wg, rhs))
        mixed_rows.append(jnp.concatenate(outs, axis=1) + bias)
    mixed = jnp.concatenate(mixed_rows, axis=0)
    y_b = u * mixed * _silu(pb[:, 2 * w:3 * w])

    z = pa[:, w:2 * w] * pa[:, 2 * w:3 * w]
    zbuf[zoff:zoff + tm, :] = z
    zfull = zbuf[...]
    conv = p["conv_b"] + (_shift_rows(zfull, 2)[zoff:, :] * p["conv_w0"]
                          + _shift_rows(zfull, 1)[zoff:, :] * p["conv_w1"]
                          + z * p["conv_w2"])
    y_a = pa[:, 0:w] * conv * _silu(pa[:, 3 * w:4 * w])
    zbuf[0:zoff, :] = zbuf[tm:tm + zoff, :]

    convo_ref[0] = z[tm - (CONV_W - 1):tm, :]
    for j in range(POOL_BUF):
        r = coff + tm - POOL_BUF + j
        poolo_ref[j, pl.ds(b, 1), :] = cbuf[r:r + 1, :]
    vo_ref[0] = v[tm - CHUNK:tm, :]
    cbuf[0:coff, :] = cbuf[tm:tm + coff, :]

    da = _dot(y_a.astype(BF16), wbr_ref[0:w, :])
    db = _dot(y_b.astype(BF16), wbr_ref[w:2 * w, :])
    dc = _dot(y_c.astype(BF16), wbr_ref[2 * w:3 * w, :])
    merged = _sigmoid(pg[:, 0:d]) * da + _sigmoid(pg[:, d:2 * d]) * db + _sigmoid(pg[:, 2 * d:3 * d]) * dc
    x_new = x + gate * _dot(merged.astype(BF16), wout_ref[...])
    if final:
        ms = jnp.mean(x_new * x_new, axis=-1, keepdims=True)
        x_new = x_new * lax.rsqrt(ms + EPS) * fg_ref[...]
    xo_ref[0] = x_new


def _decode_rows(xs_ref, mods_ref, sconv_ref, spool_hbm, rows_ref, win_ref, poolw_ref, wbr_ref, wout_ref, fg_ref,
                 xso_ref, convso_ref, vso_ref, poolso_hbm, state, cx_stage, sem, *, layer, final):
    d = xs_ref.shape[1]
    w = d // 2
    p = _layer_rows(rows_ref, layer, w)
    gd = w // len(POOL_WINDOWS)

    load = pltpu.make_async_copy(spool_hbm.at[layer], state, sem.at[0])
    load.start()

    x = xs_ref[...]
    mod = mods_ref[...]
    shift, scale, gate = mod[:, 0:d], mod[:, d:2 * d], mod[:, 2 * d:3 * d]
    hb = _modulated_norm(x, p["norm_g"], shift, scale).astype(BF16)

    pa = _dot(hb, win_ref[:, 0:4 * w])
    z = pa[:, w:2 * w] * pa[:, 2 * w:3 * w]
    prev = sconv_ref[...]
    conv = p["conv_b"] + (prev[:, 0:w] * p["conv_w0"] + prev[:, w:2 * w] * p["conv_w1"] + z * p["conv_w2"])
    y_a = pa[:, 0:w] * conv * _silu(pa[:, 3 * w:4 * w])
    convso_ref[:, 0:w] = prev[:, w:2 * w]
    convso_ref[:, w:2 * w] = z

    pb = _dot(hb, win_ref[:, 4 * w:7 * w])
    u = _gelu(pb[:, 0:w])
    v = _layernorm(_gelu(pb[:, w:2 * w]), p["lnv_g"], p["lnv_b"])
    vso_ref[...] = v
    y_b = u * (v * p["sgu_w0"] + p["sgu_b0"]) * _silu(pb[:, 2 * w:3 * w])

    pc = _dot(hb, win_ref[:, 7 * w:9 * w])
    c_x = pc[:, 0:w]
    cx_stage[...] = c_x
    newest = pltpu.make_async_copy(cx_stage, poolso_hbm.at[POOL_BUF - 1], sem.at[1])
    newest.start()
    load.wait()
    keep = pltpu.make_async_copy(state.at[pl.ds(1, POOL_BUF - 1)], poolso_hbm.at[pl.ds(0, POOL_BUF - 1)], sem.at[2])
    keep.start()
    yc_groups = []
    for g, win in enumerate(POOL_WINDOWS):
        s = c_x[:, g * gd:(g + 1) * gd]
        for j in range(POOL_BUF - (win - 1), POOL_BUF):
            s = s + state[j, :, g * gd:(g + 1) * gd]
        pm = s / float(min(win, PAST_LEN + 1)) - c_x[:, g * gd:(g + 1) * gd]
        yc_groups.append(_dot(pm.astype(BF16), poolw_ref[g].astype(BF16)))
    y_c = jnp.concatenate(yc_groups, axis=1)
    y_c = (y_c + p["pool_b"]) * p["pool_scale"] * _silu(pc[:, w:2 * w])

    pg = _dot(hb, win_ref[:, 9 * w:9 * w + 3 * d])
    merged = (_sigmoid(pg[:, 0:d]) * _dot(y_a.astype(BF16), wbr_ref[0:w, :])
              + _sigmoid(pg[:, d:2 * d]) * _dot(y_b.astype(BF16), wbr_ref[w:2 * w, :])
              + _sigmoid(pg[:, 2 * d:3 * d]) * _dot(y_c.astype(BF16), wbr_ref[2 * w:3 * w, :]))
    x_new = x + gate * _dot(merged.astype(BF16), wout_ref[...])
    if final:
        ms = jnp.mean(x_new * x_new, axis=-1, keepdims=True)
        x_new = x_new * lax.rsqrt(ms + EPS) * fg_ref[...]
    xso_ref[...] = x_new
    newest.wait()
    keep.wait()


def _layer_kernel(x_ref, mod_ref, rows_ref, win_ref, sguw_ref, sgub_ref, poolw_ref, wbr_ref, wout_ref, fg_ref,
                  xs_ref, mods_ref, sconv_ref, spool_hbm,
                  *rest, layer, final, n_tiles, prompt_steps):
    if final:
        outs, scratch = rest[:8], rest[8:]
    else:
        casts_in, outs, casts_out, scratch = rest[:3], rest[3:11], rest[11:14], rest[14:]
    xo_ref, convo_ref, poolo_ref, vo_ref, xso_ref, convso_ref, vso_ref, poolso_hbm = outs
    zbuf, cbuf, state, cx_stage, sem = scratch
    step = pl.program_id(0)

    @pl.when(step < prompt_steps)
    def _():
        if not final:
            for src, dst in zip(casts_in, casts_out):
                dst[...] = src[...].astype(BF16)
        _prompt_tile(step // n_tiles, step % n_tiles, x_ref, mod_ref, rows_ref, win_ref, sguw_ref, sgub_ref,
                     poolw_ref, wbr_ref, wout_ref, fg_ref, xo_ref, convo_ref, poolo_ref, vo_ref, zbuf, cbuf,
                     layer=layer, final=final)

    @pl.when(step == prompt_steps)
    def _():
        _decode_rows(xs_ref, mods_ref, sconv_ref, spool_hbm, rows_ref, win_ref, poolw_ref, wbr_ref, wout_ref, fg_ref,
                     xso_ref, convso_ref, vso_ref, poolso_hbm, state, cx_stage, sem, layer=layer, final=final)


def _layer_call(layer, final, x, xs, mod_p, mod_s, sconv, spool, rows_p, sgub, fg, weights, sguw, poolw, next_f32):
    n, seq, d = x.shape
    rows = xs.shape[0]
    w = d // 2
    tm = SEQ_TILE
    n_tiles = seq // tm
    prompt_steps = n * n_tiles
    resident = dict(pipeline_mode=pl.Buffered(1))
    whole = lambda a: pl.BlockSpec(a.shape, lambda s: (0,) * a.ndim, **resident)
    of_layer = lambda a: pl.BlockSpec((None,) + a.shape[1:], lambda s: (layer,) + (0,) * (a.ndim - 1), **resident)
    win, wbr, wout = weights

    ptile = lambda s: jnp.minimum(s, prompt_steps - 1)
    x_tile = lambda s: (ptile(s) // n_tiles, ptile(s) % n_tiles, 0)
    batch_row = lambda s: (ptile(s) // n_tiles, 0, 0)

    in_specs = [
        pl.BlockSpec((1, tm, d), x_tile),
        whole(mod_p), whole(rows_p), whole(win), of_layer(sguw), whole(sgub), of_layer(poolw), whole(wbr),
        whole(wout), whole(fg),
        whole(xs), of_layer(mod_s), of_layer(sconv), pl.BlockSpec(memory_space=pl.ANY),
    ]
    out_specs = [
        pl.BlockSpec((1, tm, d), x_tile),
        pl.BlockSpec((1, CONV_W - 1, w), batch_row),
        pl.BlockSpec((POOL_BUF, n, w), lambda s: (0, 0, 0)),
        pl.BlockSpec((1, CHUNK, w), batch_row),
        pl.BlockSpec((rows, d), lambda s: (0, 0)),
        pl.BlockSpec((rows, (CONV_W - 1) * w), lambda s: (0, 0)),
        pl.BlockSpec((rows, w), lambda s: (0, 0)),
        pl.BlockSpec(memory_space=pl.ANY),
    ]
    out_shape = [
        jax.ShapeDtypeStruct((n, seq, d), F32),
        jax.ShapeDtypeStruct((n, CONV_W - 1, w), F32),
        jax.ShapeDtypeStruct((POOL_BUF, n, w), F32),
        jax.ShapeDtypeStruct((n, CHUNK, w), F32),
        jax.ShapeDtypeStruct((rows, d), F32),
        jax.ShapeDtypeStruct((rows, (CONV_W - 1) * w), F32),
        jax.ShapeDtypeStruct((rows, w), F32),
        jax.ShapeDtypeStruct((POOL_BUF, rows, w), F32),
    ]
    args = [x, mod_p, rows_p, win, sguw, sgub, poolw, wbr, wout, fg, xs, mod_s, sconv, spool]
    if not final:
        for src in next_f32:
            nrows, cols = src.shape[1:]
            assert nrows % prompt_steps == 0 and (nrows // prompt_steps) % BF16_SUBLANES == 0
            blk = nrows // prompt_steps
            in_specs.append(pl.BlockSpec((None, blk, cols), lambda s: (layer + 1, ptile(s), 0)))
            out_specs.append(pl.BlockSpec((blk, cols), lambda s: (ptile(s), 0)))
            out_shape.append(jax.ShapeDtypeStruct((nrows, cols), BF16))
            args.append(src)
    scratch_shapes = [
        pltpu.VMEM((SUBLANES + tm, w), F32),
        pltpu.VMEM((2 * SUBLANES + tm, w), F32),
        pltpu.VMEM((POOL_BUF, rows, w), F32),
        pltpu.VMEM((rows, w), F32),
        pltpu.SemaphoreType.DMA((3,)),
    ]
    return pl.pallas_call(
        functools.partial(_layer_kernel, layer=layer, final=final, n_tiles=n_tiles, prompt_steps=prompt_steps),
        grid=(prompt_steps + 1,),
        in_specs=in_specs,
        out_specs=out_specs,
        out_shape=out_shape,
        scratch_shapes=scratch_shapes,
        compiler_params=pltpu.CompilerParams(
            dimension_semantics=("arbitrary",),
            vmem_limit_bytes=VMEM_LIMIT_BYTES),
        name=f"layer{layer}",
    )(*args)


def kernel(x_prompt, x_sample, c_prompt, c_sample, state_conv, state_pool, w_ada, b_ada, norm_g, w_in, conv_w,
           conv_b, lnv_g, lnv_b, sgu_w, sgu_b, pool_w, pool_b, pool_scale, w_branch, w_out, final_g):
    n, seq, d = x_prompt.shape
    rows = x_sample.shape[0]
    depth = w_in.shape[0]
    w = d // 2
    hd = w // SGU_HEADS
    assert x_sample.shape[1] == 1 and seq % SEQ_TILE == 0 and SEQ_TILE % CHUNK == 0

    next_f32 = (w_in, w_branch.reshape(depth, N_BRANCH * w, d), w_out)
    mod_p, mod_s, *weights = _mod_call(c_prompt, c_sample, w_ada, b_ada, next_f32)

    spread = lambda a: jnp.repeat(a, hd, axis=1)
    rows_p = jnp.concatenate([norm_g, conv_b, lnv_g, lnv_b, pool_b, pool_scale, spread(sgu_w[:, :, 0, 0]),
                              spread(sgu_b[:, :, 0]), conv_w[:, 0], conv_w[:, 1], conv_w[:, 2]], axis=1)[:, None, :]
    sgub_full = jnp.repeat(jnp.swapaxes(sgu_b, 1, 2), hd, axis=2)
    fg = final_g.reshape(1, d)
    hist_major = lambda a: jnp.transpose(a, (0, 2, 1, 3))
    sconv = state_conv.reshape(depth, rows, (CONV_W - 1) * w)
    spool = hist_major(state_pool)

    xp, xs = x_prompt, x_sample.reshape(rows, d)
    conv_p, pool_p, v_p, conv_s, pool_s, v_s = [], [], [], [], [], []
    for l in range(depth):
        xp, cp, pp, vp, xs, cs, vs, ps, *next_weights = _layer_call(
            l, l == depth - 1, xp, xs, mod_p, mod_s, sconv, spool, rows_p, sgub_full, fg, tuple(weights), sgu_w,
            pool_w, next_f32)
        conv_p.append(cp); pool_p.append(pp); v_p.append(vp)
        conv_s.append(cs); pool_s.append(ps); v_s.append(vs)
        weights = next_weights

    conv_s = jnp.stack(conv_s).reshape(depth, rows, CONV_W - 1, w)
    v_s = jnp.stack(v_s).reshape(depth, rows, 1, w)
    return (xp, xs.reshape(rows, 1, d), jnp.stack(conv_p), conv_s, hist_major(jnp.stack(pool_p)),
            hist_major(jnp.stack(pool_s)), jnp.stack(v_p), v_s)
```

```python
import functools
import math

import jax
import jax.numpy as jnp
from jax import lax
from jax.experimental import pallas as pl
from jax.experimental.pallas import tpu as pltpu

F32 = jnp.float32
BF16 = jnp.bfloat16

CONV_W = 3
CHUNK = 128
SGU_HEADS = 8
POOL_WINDOWS = (2, 4, 8, 16)
POOL_BUF = max(POOL_WINDOWS) - 1
PAST_LEN = 16384
EPS = 1e-6

N_BRANCH = 3
N_STATE_OUTPUTS = 6
SUBLANES = 8
BF16_SUBLANES = 16
SEQ_TILE = 512
HEADS_PER_DOT = 2
VMEM_LIMIT_BYTES = 58 * 1024 * 1024


def _sigmoid(x):
    return 0.5 * (1.0 + jnp.tanh(0.5 * x))


def _silu(x):
    return x * _sigmoid(x)


def _gelu(x):
    c = math.sqrt(2.0 / math.pi)
    return 0.5 * x * (1.0 + jnp.tanh(c * (x + 0.044715 * (x * x * x))))


def _shift_rows(x, k):
    return pltpu.roll(x, k, axis=0)


def _dot(a, b):
    return jnp.dot(a, b, preferred_element_type=F32)


def _modulated_norm(x, norm_g, shift, scale):
    ms = jnp.mean(x * x, axis=-1, keepdims=True)
    return (x * lax.rsqrt(ms + EPS)) * (norm_g * (1.0 + scale)) + shift


_ROW_FIELDS = (("norm_g", 2), ("conv_b", 1), ("lnv_g", 1), ("lnv_b", 1), ("pool_b", 1), ("pool_scale", 1),
               ("sgu_w0", 1), ("sgu_b0", 1), ("conv_w0", 1), ("conv_w1", 1), ("conv_w2", 1))


def _layer_rows(rows_ref, layer, w):
    out, lo = {}, 0
    for name, units in _ROW_FIELDS:
        out[name] = rows_ref[layer, :, lo:lo + units * w]
        lo += units * w
    return out


def _layernorm(x, g, b):
    mu = jnp.mean(x, axis=-1, keepdims=True)
    d = x - mu
    var = jnp.mean(d * d, axis=-1, keepdims=True)
    return d * lax.rsqrt(var + EPS) * g + b


def _mod_kernel(cp_ref, cs_ref, w_ref, b_ref, *rest, n_cast):
    casts_in, (op_ref, os_ref), rest = rest[:n_cast], rest[n_cast:n_cast + 2], rest[n_cast + 2:]
    casts_out, state_refs = rest[:n_cast], rest[n_cast:]
    for ref in state_refs:
        ref[...] = jnp.zeros(ref.shape, ref.dtype)
    n = cp_ref.shape[0]
    c = jnp.concatenate([cp_ref[...], cp_ref[...], cs_ref[...]], axis=0)
    m = _dot(_silu(c).astype(BF16), w_ref[0].astype(BF16)) + b_ref[0]
    op_ref[0] = m[0:n]
    os_ref[0] = m[2 * n:]
    for src, dst in zip(casts_in, casts_out):
        dst[...] = src[...].astype(BF16)


def _mod_call(c_prompt, c_sample, w_ada, b_ada, first_f32, state_shapes):
    depth, d, d3 = w_ada.shape
    n, rows = c_prompt.shape[0], c_sample.shape[0]
    assert n == SUBLANES
    n_col = 2
    col_block = d3 // n_col
    steps = depth * n_col
    in_specs = [
        pl.BlockSpec((n, d), lambda l, j: (0, 0)),
        pl.BlockSpec((rows, d), lambda l, j: (0, 0)),
        pl.BlockSpec((1, d, col_block), lambda l, j: (l, 0, j)),
        pl.BlockSpec((1, 1, col_block), lambda l, j: (l, 0, j)),
    ]
    out_specs = [
        pl.BlockSpec((1, n, col_block), lambda l, j: (l, 0, j)),
        pl.BlockSpec((1, rows, col_block), lambda l, j: (l, 0, j)),
    ]
    out_shape = [
        jax.ShapeDtypeStruct((depth, n, d3), F32),
        jax.ShapeDtypeStruct((depth, rows, d3), F32),
    ]
    for src in first_f32:
        nrows, cols = src.shape[1:]
        assert nrows % steps == 0 and (nrows // steps) % BF16_SUBLANES == 0
        blk = nrows // steps
        in_specs.append(pl.BlockSpec((None, blk, cols), lambda l, j: (0, l * n_col + j, 0)))
        out_specs.append(pl.BlockSpec((blk, cols), lambda l, j: (l * n_col + j, 0)))
        out_shape.append(jax.ShapeDtypeStruct((nrows, cols), BF16))
    for shape in state_shapes:
        assert shape[0] == depth
        out_specs.append(pl.BlockSpec((1,) + shape[1:], lambda l, j, nd=len(shape): (l,) + (0,) * (nd - 1)))
        out_shape.append(jax.ShapeDtypeStruct(shape, F32))
    return pl.pallas_call(
        functools.partial(_mod_kernel, n_cast=len(first_f32)),
        grid=(depth, n_col),
        in_specs=in_specs,
        out_specs=out_specs,
        out_shape=out_shape,
        compiler_params=pltpu.CompilerParams(
            dimension_semantics=("arbitrary", "arbitrary"),
            vmem_limit_bytes=VMEM_LIMIT_BYTES),
        name="adaln_mod",
    )(c_prompt, c_sample, w_ada, b_ada.reshape(depth, 1, d3), *first_f32)


def _prompt_tile(b, t, x_ref, mod_ref, rows_ref, win_ref, sguw_ref, sgub_ref, poolw_ref, wbr_ref, wout_ref, fg_ref,
                 xo_ref, convo_ref, poolo_ref, vo_ref, zbuf, cbuf, *, layer, final):
    tm, d = x_ref.shape[1], x_ref.shape[2]
    w = d // 2
    p = _layer_rows(rows_ref, layer, w)
    zoff = SUBLANES
    coff = 2 * SUBLANES

    @pl.when(t == 0)
    def _():
        zbuf[0:zoff, :] = jnp.zeros((zoff, w), F32)
        cbuf[0:coff, :] = jnp.zeros((coff, w), F32)

    x = x_ref[0]
    mod = mod_ref[layer, pl.ds(b, 1), :]
    shift, scale, gate = mod[:, 0:d], mod[:, d:2 * d], mod[:, 2 * d:3 * d]
    hb = _modulated_norm(x, p["norm_g"], shift, scale).astype(BF16)

    pc = _dot(hb, win_ref[:, 7 * w:9 * w])
    pb = _dot(hb, win_ref[:, 4 * w:7 * w])
    pa = _dot(hb, win_ref[:, 0:4 * w])
    pg = _dot(hb, win_ref[:, 9 * w:9 * w + 3 * d])

    c_x = pc[:, 0:w]
    cbuf[coff:coff + tm, :] = c_x
    gd = w // len(POOL_WINDOWS)
    pos1 = lax.broadcasted_iota(jnp.int32, (tm, gd), 0) + (t * tm + 1)
    yc_groups = []
    for g, win in enumerate(POOL_WINDOWS):
        cols = slice(g * gd, (g + 1) * gd)
        s = cbuf[:, cols]
        k = 1
        while k < win:
            s = s + _shift_rows(s, k)
            k *= 2
        s = s[coff:, :]
        cnt = jnp.minimum(pos1, win).astype(F32)
        pm = s / cnt - c_x[:, cols]
        yc_groups.append(_dot(pm.astype(BF16), poolw_ref[g].astype(BF16)))
    y_c = jnp.concatenate(yc_groups, axis=1)
    y_c = (y_c + p["pool_b"]) * p["pool_scale"] * _silu(pc[:, w:2 * w])

    u = _gelu(pb[:, 0:w])
    v = _layernorm(_gelu(pb[:, w:2 * w]), p["lnv_g"], p["lnv_b"])
    vb = v.astype(BF16)
    causal = (lax.broadcasted_iota(jnp.int32, (CHUNK, CHUNK), 1)
              <= lax.broadcasted_iota(jnp.int32, (CHUNK, CHUNK), 0))
    hd = w // SGU_HEADS
    gcols = HEADS_PER_DOT * hd
    lane_head = lax.broadcasted_iota(jnp.int32, (CHUNK, gcols), 1) // hd
    bias = sgub_ref[layer]
    wgs = [jnp.concatenate([jnp.where(causal, sguw_ref[g * HEADS_PER_DOT + hh], 0.0).astype(BF16)
                            for hh in range(HEADS_PER_DOT)], axis=1)
           for g in range(SGU_HEADS // HEADS_PER_DOT)]
    mixed_rows = []
    for c in range(tm // CHUNK):
        vc = vb[c * CHUNK:(c + 1) * CHUNK, :]
        outs = []
        for g, wg in enumerate(wgs):
            vg = vc[:, g * gcols:(g + 1) * gcols]
            rhs = jnp.concatenate(
                [jnp.where(lane_head == hh, vg, jnp.zeros((), BF16)) for hh in range(HEADS_PER_DOT)], axis=0)
            outs.append(_dot(wg, rhs))
        mixed_rows.append(jnp.concatenate(outs, axis=1) + bias)
    mixed = jnp.concatenate(mixed_rows, axis=0)
    y_b = u * mixed * _silu(pb[:, 2 * w:3 * w])

    z = pa[:, w:2 * w] * pa[:, 2 * w:3 * w]
    zbuf[zoff:zoff + tm, :] = z
    zfull = zbuf[...]
    conv = p["conv_b"] + (_shift_rows(zfull, 2)[zoff:, :] * p["conv_w0"]
                          + _shift_rows(zfull, 1)[zoff:, :] * p["conv_w1"]
                          + z * p["conv_w2"])
    y_a = pa[:, 0:w] * conv * _silu(pa[:, 3 * w:4 * w])
    zbuf[0:zoff, :] = zbuf[tm:tm + zoff, :]

    convo_ref[0] = z[tm - (CONV_W - 1):tm, :]
    for j in range(POOL_BUF):
        r = coff + tm - POOL_BUF + j
        poolo_ref[j, pl.ds(b, 1), :] = cbuf[r:r + 1, :]
    vo_ref[0] = v[tm - CHUNK:tm, :]
    cbuf[0:coff, :] = cbuf[tm:tm + coff, :]

    da = _dot(y_a.astype(BF16), wbr_ref[0:w, :])
    db = _dot(y_b.astype(BF16), wbr_ref[w:2 * w, :])
    dc = _dot(y_c.astype(BF16), wbr_ref[2 * w:3 * w, :])
    merged = _sigmoid(pg[:, 0:d]) * da + _sigmoid(pg[:, d:2 * d]) * db + _sigmoid(pg[:, 2 * d:3 * d]) * dc
    x_new = x + gate * _dot(merged.astype(BF16), wout_ref[...])
    if final:
        ms = jnp.mean(x_new * x_new, axis=-1, keepdims=True)
        x_new = x_new * lax.rsqrt(ms + EPS) * fg_ref[...]
    xo_ref[0] = x_new


def _decode_rows(xs_ref, mods_ref, sconv_ref, spool_hbm, rows_ref, win_ref, poolw_ref, wbr_ref, wout_ref, fg_ref,
                 xso_ref, convso_ref, vso_ref, poolso_hbm, state, cx_stage, sem, *, layer, final):
    d = xs_ref.shape[1]
    w = d // 2
    p = _layer_rows(rows_ref, layer, w)
    gd = w // len(POOL_WINDOWS)

    load = pltpu.make_async_copy(spool_hbm.at[layer], state, sem.at[0])
    load.start()

    x = xs_ref[...]
    mod = mods_ref[...]
    shift, scale, gate = mod[:, 0:d], mod[:, d:2 * d], mod[:, 2 * d:3 * d]
    hb = _modulated_norm(x, p["norm_g"], shift, scale).astype(BF16)

    pa = _dot(hb, win_ref[:, 0:4 * w])
    z = pa[:, w:2 * w] * pa[:, 2 * w:3 * w]
    prev = sconv_ref[...]
    conv = p["conv_b"] + (prev[:, 0:w] * p["conv_w0"] + prev[:, w:2 * w] * p["conv_w1"] + z * p["conv_w2"])
    y_a = pa[:, 0:w] * conv * _silu(pa[:, 3 * w:4 * w])
    convso_ref[:, 0:w] = prev[:, w:2 * w]
    convso_ref[:, w:2 * w] = z

    pb = _dot(hb, win_ref[:, 4 * w:7 * w])
    u = _gelu(pb[:, 0:w])
    v = _layernorm(_gelu(pb[:, w:2 * w]), p["lnv_g"], p["lnv_b"])
    vso_ref[...] = v
    y_b = u * (v * p["sgu_w0"] + p["sgu_b0"]) * _silu(pb[:, 2 * w:3 * w])

    pc = _dot(hb, win_ref[:, 7 * w:9 * w])
    c_x = pc[:, 0:w]
    cx_stage[...] = c_x
    newest = pltpu.make_async_copy(cx_stage, poolso_hbm.at[layer, POOL_BUF - 1], sem.at[1])
    newest.start()
    load.wait()
    keep = pltpu.make_async_copy(state.at[pl.ds(1, POOL_BUF - 1)], poolso_hbm.at[layer, pl.ds(0, POOL_BUF - 1)],
                                 sem.at[2])
    keep.start()
    yc_groups = []
    for g, win in enumerate(POOL_WINDOWS):
        s = c_x[:, g * gd:(g + 1) * gd]
        for j in range(POOL_BUF - (win - 1), POOL_BUF):
            s = s + state[j, :, g * gd:(g + 1) * gd]
        pm = s / float(min(win, PAST_LEN + 1)) - c_x[:, g * gd:(g + 1) * gd]
        yc_groups.append(_dot(pm.astype(BF16), poolw_ref[g].astype(BF16)))
    y_c = jnp.concatenate(yc_groups, axis=1)
    y_c = (y_c + p["pool_b"]) * p["pool_scale"] * _silu(pc[:, w:2 * w])

    pg = _dot(hb, win_ref[:, 9 * w:9 * w + 3 * d])
    merged = (_sigmoid(pg[:, 0:d]) * _dot(y_a.astype(BF16), wbr_ref[0:w, :])
              + _sigmoid(pg[:, d:2 * d]) * _dot(y_b.astype(BF16), wbr_ref[w:2 * w, :])
              + _sigmoid(pg[:, 2 * d:3 * d]) * _dot(y_c.astype(BF16), wbr_ref[2 * w:3 * w, :]))
    x_new = x + gate * _dot(merged.astype(BF16), wout_ref[...])
    if final:
        ms = jnp.mean(x_new * x_new, axis=-1, keepdims=True)
        x_new = x_new * lax.rsqrt(ms + EPS) * fg_ref[...]
    xso_ref[...] = x_new
    newest.wait()
    keep.wait()


def _layer_kernel(x_ref, mod_ref, rows_ref, win_ref, sguw_ref, sgub_ref, poolw_ref, wbr_ref, wout_ref, fg_ref,
                  xs_ref, mods_ref, sconv_ref, spool_hbm,
                  *rest, layer, final, n_tiles, prompt_steps):
    rest = rest[N_STATE_OUTPUTS:]
    if final:
        outs, scratch = rest[:8], rest[8:]
    else:
        casts_in, outs, casts_out, scratch = rest[:3], rest[3:11], rest[11:14], rest[14:]
    xo_ref, convo_ref, poolo_ref, vo_ref, xso_ref, convso_ref, vso_ref, poolso_hbm = outs
    zbuf, cbuf, state, cx_stage, sem = scratch
    step = pl.program_id(0)

    @pl.when(step < prompt_steps)
    def _():
        if not final:
            for src, dst in zip(casts_in, casts_out):
                dst[...] = src[...].astype(BF16)
        _prompt_tile(step // n_tiles, step % n_tiles, x_ref, mod_ref, rows_ref, win_ref, sguw_ref, sgub_ref,
                     poolw_ref, wbr_ref, wout_ref, fg_ref, xo_ref, convo_ref, poolo_ref, vo_ref, zbuf, cbuf,
                     layer=layer, final=final)

    @pl.when(step == prompt_steps)
    def _():
        _decode_rows(xs_ref, mods_ref, sconv_ref, spool_hbm, rows_ref, win_ref, poolw_ref, wbr_ref, wout_ref, fg_ref,
                     xso_ref, convso_ref, vso_ref, poolso_hbm, state, cx_stage, sem, layer=layer, final=final)


def _layer_call(layer, final, x, xs, mod_p, mod_s, sconv, spool, rows_p, sgub, fg, weights, sguw, poolw, next_f32,
                states):
    n, seq, d = x.shape
    rows = xs.shape[0]
    w = d // 2
    tm = SEQ_TILE
    n_tiles = seq // tm
    prompt_steps = n * n_tiles
    resident = dict(pipeline_mode=pl.Buffered(1))
    whole = lambda a: pl.BlockSpec(a.shape, lambda s: (0,) * a.ndim, **resident)
    of_layer = lambda a: pl.BlockSpec((None,) + a.shape[1:], lambda s: (layer,) + (0,) * (a.ndim - 1), **resident)
    win, wbr, wout = weights

    ptile = lambda s: jnp.minimum(s, prompt_steps - 1)
    x_tile = lambda s: (ptile(s) // n_tiles, ptile(s) % n_tiles, 0)
    batch_row = lambda s: (layer, ptile(s) // n_tiles, 0, 0)
    hbm = pl.BlockSpec(memory_space=pl.ANY)

    in_specs = [
        pl.BlockSpec((1, tm, d), x_tile),
        whole(mod_p), whole(rows_p), whole(win), of_layer(sguw), whole(sgub), of_layer(poolw), whole(wbr),
        whole(wout), whole(fg),
        whole(xs), of_layer(mod_s), of_layer(sconv), hbm,
    ]
    out_specs = [
        pl.BlockSpec((1, tm, d), x_tile),
        pl.BlockSpec((None, 1, CONV_W - 1, w), batch_row),
        pl.BlockSpec((None, POOL_BUF, n, w), lambda s: (layer, 0, 0, 0)),
        pl.BlockSpec((None, 1, CHUNK, w), batch_row),
        pl.BlockSpec((rows, d), lambda s: (0, 0)),
        pl.BlockSpec((None, rows, (CONV_W - 1) * w), lambda s: (layer, 0, 0)),
        pl.BlockSpec((None, rows, w), lambda s: (layer, 0, 0)),
        hbm,
    ]
    conv_p, pool_p, v_p, conv_s, v_s, pool_s = (jax.ShapeDtypeStruct(a.shape, F32) for a in states)
    out_shape = [jax.ShapeDtypeStruct((n, seq, d), F32), conv_p, pool_p, v_p,
                 jax.ShapeDtypeStruct((rows, d), F32), conv_s, v_s, pool_s]
    state_outputs = (1, 2, 3, 5, 6, 7)
    assert len(state_outputs) == len(states) == N_STATE_OUTPUTS
    args = [x, mod_p, rows_p, win, sguw, sgub, poolw, wbr, wout, fg, xs, mod_s, sconv, spool]
    aliases = {len(args) + i: o for i, o in enumerate(state_outputs)}
    in_specs += [hbm] * len(states)
    args += list(states)
    if not final:
        for src in next_f32:
            nrows, cols = src.shape[1:]
            assert nrows % prompt_steps == 0 and (nrows // prompt_steps) % BF16_SUBLANES == 0
            blk = nrows // prompt_steps
            in_specs.append(pl.BlockSpec((None, blk, cols), lambda s: (layer + 1, ptile(s), 0)))
            out_specs.append(pl.BlockSpec((blk, cols), lambda s: (ptile(s), 0)))
            out_shape.append(jax.ShapeDtypeStruct((nrows, cols), BF16))
            args.append(src)
    scratch_shapes = [
        pltpu.VMEM((SUBLANES + tm, w), F32),
        pltpu.VMEM((2 * SUBLANES + tm, w), F32),
        pltpu.VMEM((POOL_BUF, rows, w), F32),
        pltpu.VMEM((rows, w), F32),
        pltpu.SemaphoreType.DMA((3,)),
    ]
    return pl.pallas_call(
        functools.partial(_layer_kernel, layer=layer, final=final, n_tiles=n_tiles, prompt_steps=prompt_steps),
        grid=(prompt_steps + 1,),
        in_specs=in_specs,
        out_specs=out_specs,
        out_shape=out_shape,
        scratch_shapes=scratch_shapes,
        input_output_aliases=aliases,
        compiler_params=pltpu.CompilerParams(
            dimension_semantics=("arbitrary",),
            vmem_limit_bytes=VMEM_LIMIT_BYTES),
        name=f"layer{layer}",
    )(*args)


def kernel(x_prompt, x_sample, c_prompt, c_sample, state_conv, state_pool, w_ada, b_ada, norm_g, w_in, conv_w,
           conv_b, lnv_g, lnv_b, sgu_w, sgu_b, pool_w, pool_b, pool_scale, w_branch, w_out, final_g):
    n, seq, d = x_prompt.shape
    rows = x_sample.shape[0]
    depth = w_in.shape[0]
    w = d // 2
    hd = w // SGU_HEADS
    assert x_sample.shape[1] == 1 and seq % SEQ_TILE == 0 and SEQ_TILE % CHUNK == 0

    next_f32 = (w_in, w_branch.reshape(depth, N_BRANCH * w, d), w_out)
    state_shapes = ((depth, n, CONV_W - 1, w), (depth, POOL_BUF, n, w), (depth, n, CHUNK, w),
                    (depth, rows, (CONV_W - 1) * w), (depth, rows, w), (depth, POOL_BUF, rows, w))
    mod_p, mod_s, *made = _mod_call(c_prompt, c_sample, w_ada, b_ada, next_f32, state_shapes)
    weights, states = made[:len(next_f32)], tuple(made[len(next_f32):])

    spread = lambda a: jnp.repeat(a, hd, axis=1)
    rows_p = jnp.concatenate([norm_g, conv_b, lnv_g, lnv_b, pool_b, pool_scale, spread(sgu_w[:, :, 0, 0]),
                              spread(sgu_b[:, :, 0]), conv_w[:, 0], conv_w[:, 1], conv_w[:, 2]], axis=1)[:, None, :]
    sgub_full = jnp.repeat(jnp.swapaxes(sgu_b, 1, 2), hd, axis=2)
    fg = final_g.reshape(1, d)
    hist_major = lambda a: jnp.transpose(a, (0, 2, 1, 3))
    sconv = state_conv.reshape(depth, rows, (CONV_W - 1) * w)
    spool = hist_major(state_pool)

    xp, xs = x_prompt, x_sample.reshape(rows, d)
    for l in range(depth):
        xp, conv_p, pool_p, v_p, xs, conv_s, v_s, pool_s, *next_weights = _layer_call(
            l, l == depth - 1, xp, xs, mod_p, mod_s, sconv, spool, rows_p, sgub_full, fg, tuple(weights), sgu_w,
            pool_w, next_f32, states)
        states = (conv_p, pool_p, v_p, conv_s, v_s, pool_s)
        weights = next_weights

    return (xp, xs.reshape(rows, 1, d), conv_p, conv_s.reshape(depth, rows, CONV_W - 1, w), hist_major(pool_p),
            hist_major(pool_s), v_p, v_s.reshape(depth, rows, 1, w))
```

```python
import functools
import math

import jax
import jax.numpy as jnp
from jax import lax
from jax.experimental import pallas as pl
from jax.experimental.pallas import tpu as pltpu

F32 = jnp.float32
BF16 = jnp.bfloat16

CONV_W = 3
CHUNK = 128
SGU_HEADS = 8
POOL_WINDOWS = (2, 4, 8, 16)
POOL_BUF = max(POOL_WINDOWS) - 1
PAST_LEN = 16384
EPS = 1e-6

N_BRANCH = 3
N_ROW_PARAMS = 8
N_STATE_OUTPUTS = 6
SUBLANES = 8
BF16_SUBLANES = 16
SEQ_TILE = 512
HEADS_PER_DOT = 2
VMEM_LIMIT_BYTES = 58 * 1024 * 1024


def _sigmoid(x):
    return 0.5 * (1.0 + jnp.tanh(0.5 * x))


def _silu(x):
    return x * _sigmoid(x)


def _gelu(x):
    c = math.sqrt(2.0 / math.pi)
    return 0.5 * x * (1.0 + jnp.tanh(c * (x + 0.044715 * (x * x * x))))


def _shift_rows(x, k):
    return pltpu.roll(x, k, axis=0)


def _dot(a, b):
    return jnp.dot(a, b, preferred_element_type=F32)


def _modulated_norm(x, norm_g, shift, scale):
    ms = jnp.mean(x * x, axis=-1, keepdims=True)
    return (x * lax.rsqrt(ms + EPS)) * (norm_g * (1.0 + scale)) + shift


def _layer_rows(prm, layer):
    ng, convb, lng, lnb, poolb, pools, convw, _ = prm
    row = lambda ref: ref[layer:layer + 1, :]
    out = dict(norm_g=row(ng), conv_b=row(convb), lnv_g=row(lng), lnv_b=row(lnb), pool_b=row(poolb),
               pool_scale=row(pools))
    for k in range(CONV_W):
        out[f"conv_w{k}"] = convw[k, layer:layer + 1, :]
    return out


def _spread_heads(per_head, w):
    hd = w // len(per_head)
    lane_head = lax.broadcasted_iota(jnp.int32, (1, w), 1) // hd
    out = jnp.zeros((1, w), F32)
    for h, val in enumerate(per_head):
        out = jnp.where(lane_head == h, val, out)
    return out


def _layernorm(x, g, b):
    mu = jnp.mean(x, axis=-1, keepdims=True)
    d = x - mu
    var = jnp.mean(d * d, axis=-1, keepdims=True)
    return d * lax.rsqrt(var + EPS) * g + b


def _mod_kernel(cp_ref, cs_ref, w_ref, b_ref, *rest, n_cast):
    casts_in, (op_ref, os_ref), rest = rest[:n_cast], rest[n_cast:n_cast + 2], rest[n_cast + 2:]
    casts_out, state_refs = rest[:n_cast], rest[n_cast:]
    for ref in state_refs:
        ref[...] = jnp.zeros(ref.shape, ref.dtype)
    n = cp_ref.shape[0]
    c = jnp.concatenate([cp_ref[...], cp_ref[...], cs_ref[...]], axis=0)
    m = _dot(_silu(c).astype(BF16), w_ref[0].astype(BF16)) + b_ref[0]
    op_ref[0] = m[0:n]
    os_ref[0] = m[2 * n:]
    for src, dst in zip(casts_in, casts_out):
        dst[...] = src[...].astype(BF16)


def _mod_call(c_prompt, c_sample, w_ada, b_ada, first_f32, state_shapes):
    depth, d, d3 = w_ada.shape
    n, rows = c_prompt.shape[0], c_sample.shape[0]
    assert n == SUBLANES
    n_col = 2
    col_block = d3 // n_col
    steps = depth * n_col
    in_specs = [
        pl.BlockSpec((n, d), lambda l, j: (0, 0)),
        pl.BlockSpec((rows, d), lambda l, j: (0, 0)),
        pl.BlockSpec((1, d, col_block), lambda l, j: (l, 0, j)),
        pl.BlockSpec((1, 1, col_block), lambda l, j: (l, 0, j)),
    ]
    out_specs = [
        pl.BlockSpec((1, n, col_block), lambda l, j: (l, 0, j)),
        pl.BlockSpec((1, rows, col_block), lambda l, j: (l, 0, j)),
    ]
    out_shape = [
        jax.ShapeDtypeStruct((depth, n, d3), F32),
        jax.ShapeDtypeStruct((depth, rows, d3), F32),
    ]
    for src in first_f32:
        nrows, cols = src.shape[1:]
        assert nrows % steps == 0 and (nrows // steps) % BF16_SUBLANES == 0
        blk = nrows // steps
        in_specs.append(pl.BlockSpec((None, blk, cols), lambda l, j: (0, l * n_col + j, 0)))
        out_specs.append(pl.BlockSpec((blk, cols), lambda l, j: (l * n_col + j, 0)))
        out_shape.append(jax.ShapeDtypeStruct((nrows, cols), BF16))
    for shape in state_shapes:
        assert shape[0] == depth
        out_specs.append(pl.BlockSpec((1,) + shape[1:], lambda l, j, nd=len(shape): (l,) + (0,) * (nd - 1)))
        out_shape.append(jax.ShapeDtypeStruct(shape, F32))
    return pl.pallas_call(
        functools.partial(_mod_kernel, n_cast=len(first_f32)),
        grid=(depth, n_col),
        in_specs=in_specs,
        out_specs=out_specs,
        out_shape=out_shape,
        compiler_params=pltpu.CompilerParams(
            dimension_semantics=("arbitrary", "arbitrary"),
            vmem_limit_bytes=VMEM_LIMIT_BYTES),
        name="adaln_mod",
    )(c_prompt, c_sample, w_ada, b_ada.reshape(depth, 1, d3), *first_f32)


def _prompt_tile(b, t, x_ref, mod_ref, prm, win_ref, sguw_ref, sgub_ref, poolw_ref, wbr_ref, wout_ref, fg_ref,
                 xo_ref, convo_ref, poolo_ref, vo_ref, zbuf, cbuf, *, layer, final):
    tm, d = x_ref.shape[1], x_ref.shape[2]
    w = d // 2
    p = _layer_rows(prm, layer)
    zoff = SUBLANES
    coff = 2 * SUBLANES

    @pl.when(t == 0)
    def _():
        zbuf[0:zoff, :] = jnp.zeros((zoff, w), F32)
        cbuf[0:coff, :] = jnp.zeros((coff, w), F32)

    x = x_ref[0]
    mod = mod_ref[layer, pl.ds(b, 1), :]
    shift, scale, gate = mod[:, 0:d], mod[:, d:2 * d], mod[:, 2 * d:3 * d]
    hb = _modulated_norm(x, p["norm_g"], shift, scale).astype(BF16)

    pc = _dot(hb, win_ref[:, 7 * w:9 * w])
    pb = _dot(hb, win_ref[:, 4 * w:7 * w])
    pa = _dot(hb, win_ref[:, 0:4 * w])
    pg = _dot(hb, win_ref[:, 9 * w:9 * w + 3 * d])

    c_x = pc[:, 0:w]
    cbuf[coff:coff + tm, :] = c_x
    gd = w // len(POOL_WINDOWS)
    pos1 = lax.broadcasted_iota(jnp.int32, (tm, gd), 0) + (t * tm + 1)
    yc_groups = []
    for g, win in enumerate(POOL_WINDOWS):
        cols = slice(g * gd, (g + 1) * gd)
        s = cbuf[:, cols]
        k = 1
        while k < win:
            s = s + _shift_rows(s, k)
            k *= 2
        s = s[coff:, :]
        cnt = jnp.minimum(pos1, win).astype(F32)
        pm = s / cnt - c_x[:, cols]
        yc_groups.append(_dot(pm.astype(BF16), poolw_ref[g].astype(BF16)))
    y_c = jnp.concatenate(yc_groups, axis=1)
    y_c = (y_c + p["pool_b"]) * p["pool_scale"] * _silu(pc[:, w:2 * w])

    u = _gelu(pb[:, 0:w])
    v = _layernorm(_gelu(pb[:, w:2 * w]), p["lnv_g"], p["lnv_b"])
    vb = v.astype(BF16)
    causal = (lax.broadcasted_iota(jnp.int32, (CHUNK, CHUNK), 1)
              <= lax.broadcasted_iota(jnp.int32, (CHUNK, CHUNK), 0))
    hd = w // SGU_HEADS
    gcols = HEADS_PER_DOT * hd
    lane_head = lax.broadcasted_iota(jnp.int32, (CHUNK, gcols), 1) // hd
    bias = sgub_ref[layer]
    wgs = [jnp.concatenate([jnp.where(causal, sguw_ref[g * HEADS_PER_DOT + hh], 0.0).astype(BF16)
                            for hh in range(HEADS_PER_DOT)], axis=1)
           for g in range(SGU_HEADS // HEADS_PER_DOT)]
    mixed_rows = []
    for c in range(tm // CHUNK):
        vc = vb[c * CHUNK:(c + 1) * CHUNK, :]
        outs = []
        for g, wg in enumerate(wgs):
            vg = vc[:, g * gcols:(g + 1) * gcols]
            rhs = jnp.concatenate(
                [jnp.where(lane_head == hh, vg, jnp.zeros((), BF16)) for hh in range(HEADS_PER_DOT)], axis=0)
            outs.append(_dot(wg, rhs))
        mixed_rows.append(jnp.concatenate(outs, axis=1) + bias)
    mixed = jnp.concatenate(mixed_rows, axis=0)
    y_b = u * mixed * _silu(pb[:, 2 * w:3 * w])

    z = pa[:, w:2 * w] * pa[:, 2 * w:3 * w]
    zbuf[zoff:zoff + tm, :] = z
    zfull = zbuf[...]
    conv = p["conv_b"] + (_shift_rows(zfull, 2)[zoff:, :] * p["conv_w0"]
                          + _shift_rows(zfull, 1)[zoff:, :] * p["conv_w1"]
                          + z * p["conv_w2"])
    y_a = pa[:, 0:w] * conv * _silu(pa[:, 3 * w:4 * w])
    zbuf[0:zoff, :] = zbuf[tm:tm + zoff, :]

    convo_ref[0] = z[tm - (CONV_W - 1):tm, :]
    for j in range(POOL_BUF):
        r = coff + tm - POOL_BUF + j
        poolo_ref[j, pl.ds(b, 1), :] = cbuf[r:r + 1, :]
    vo_ref[0] = v[tm - CHUNK:tm, :]
    cbuf[0:coff, :] = cbuf[tm:tm + coff, :]

    da = _dot(y_a.astype(BF16), wbr_ref[0:w, :])
    db = _dot(y_b.astype(BF16), wbr_ref[w:2 * w, :])
    dc = _dot(y_c.astype(BF16), wbr_ref[2 * w:3 * w, :])
    merged = _sigmoid(pg[:, 0:d]) * da + _sigmoid(pg[:, d:2 * d]) * db + _sigmoid(pg[:, 2 * d:3 * d]) * dc
    x_new = x + gate * _dot(merged.astype(BF16), wout_ref[...])
    if final:
        ms = jnp.mean(x_new * x_new, axis=-1, keepdims=True)
        x_new = x_new * lax.rsqrt(ms + EPS) * fg_ref[...]
    xo_ref[0] = x_new


def _decode_rows(xs_ref, mods_ref, sconv_ref, spool_hbm, prm, sguw_ref, win_ref, poolw_ref, wbr_ref, wout_ref, fg_ref,
                 xso_ref, convso_ref, vso_ref, poolso_hbm, state, cx_stage, sem, *, layer, final):
    d = xs_ref.shape[1]
    w = d // 2
    p = _layer_rows(prm, layer)
    gd = w // len(POOL_WINDOWS)

    load = pltpu.make_async_copy(spool_hbm.at[layer], state, sem.at[0])
    load.start()

    x = xs_ref[...]
    mod = mods_ref[...]
    shift, scale, gate = mod[:, 0:d], mod[:, d:2 * d], mod[:, 2 * d:3 * d]
    hb = _modulated_norm(x, p["norm_g"], shift, scale).astype(BF16)

    pa = _dot(hb, win_ref[:, 0:4 * w])
    z = pa[:, w:2 * w] * pa[:, 2 * w:3 * w]
    prev = sconv_ref[...]
    conv = p["conv_b"] + (prev[:, 0:w] * p["conv_w0"] + prev[:, w:2 * w] * p["conv_w1"] + z * p["conv_w2"])
    y_a = pa[:, 0:w] * conv * _silu(pa[:, 3 * w:4 * w])
    convso_ref[:, 0:w] = prev[:, w:2 * w]
    convso_ref[:, w:2 * w] = z

    pb = _dot(hb, win_ref[:, 4 * w:7 * w])
    u = _gelu(pb[:, 0:w])
    v = _layernorm(_gelu(pb[:, w:2 * w]), p["lnv_g"], p["lnv_b"])
    vso_ref[...] = v
    sgu_w0 = _spread_heads([sguw_ref[h, 0:1, 0:1] for h in range(SGU_HEADS)], w)
    sgu_b0 = _spread_heads([prm[-1][layer, h:h + 1, 0:1] for h in range(SGU_HEADS)], w)
    y_b = u * (v * sgu_w0 + sgu_b0) * _silu(pb[:, 2 * w:3 * w])

    pc = _dot(hb, win_ref[:, 7 * w:9 * w])
    c_x = pc[:, 0:w]
    cx_stage[...] = c_x
    newest = pltpu.make_async_copy(cx_stage, poolso_hbm.at[layer, POOL_BUF - 1], sem.at[1])
    newest.start()
    load.wait()
    keep = pltpu.make_async_copy(state.at[pl.ds(1, POOL_BUF - 1)], poolso_hbm.at[layer, pl.ds(0, POOL_BUF - 1)],
                                 sem.at[2])
    keep.start()
    yc_groups = []
    for g, win in enumerate(POOL_WINDOWS):
        s = c_x[:, g * gd:(g + 1) * gd]
        for j in range(POOL_BUF - (win - 1), POOL_BUF):
            s = s + state[j, :, g * gd:(g + 1) * gd]
        pm = s / float(min(win, PAST_LEN + 1)) - c_x[:, g * gd:(g + 1) * gd]
        yc_groups.append(_dot(pm.astype(BF16), poolw_ref[g].astype(BF16)))
    y_c = jnp.concatenate(yc_groups, axis=1)
    y_c = (y_c + p["pool_b"]) * p["pool_scale"] * _silu(pc[:, w:2 * w])

    pg = _dot(hb, win_ref[:, 9 * w:9 * w + 3 * d])
    merged = (_sigmoid(pg[:, 0:d]) * _dot(y_a.astype(BF16), wbr_ref[0:w, :])
              + _sigmoid(pg[:, d:2 * d]) * _dot(y_b.astype(BF16), wbr_ref[w:2 * w, :])
              + _sigmoid(pg[:, 2 * d:3 * d]) * _dot(y_c.astype(BF16), wbr_ref[2 * w:3 * w, :]))
    x_new = x + gate * _dot(merged.astype(BF16), wout_ref[...])
    if final:
        ms = jnp.mean(x_new * x_new, axis=-1, keepdims=True)
        x_new = x_new * lax.rsqrt(ms + EPS) * fg_ref[...]
    xso_ref[...] = x_new
    newest.wait()
    keep.wait()


def _layer_kernel(x_ref, mod_ref, win_ref, sguw_ref, sgub_ref, poolw_ref, wbr_ref, wout_ref, fg_ref,
                  xs_ref, mods_ref, sconv_ref, spool_hbm,
                  *rest, layer, final, n_tiles, prompt_steps):
    prm, rest = rest[:N_ROW_PARAMS], rest[N_ROW_PARAMS:]
    rest = rest[N_STATE_OUTPUTS:]
    if final:
        outs, scratch = rest[:8], rest[8:]
    else:
        casts_in, outs, casts_out, scratch = rest[:3], rest[3:11], rest[11:14], rest[14:]
    xo_ref, convo_ref, poolo_ref, vo_ref, xso_ref, convso_ref, vso_ref, poolso_hbm = outs
    zbuf, cbuf, state, cx_stage, sem = scratch
    step = pl.program_id(0)

    @pl.when(step < prompt_steps)
    def _():
        if not final:
            for src, dst in zip(casts_in, casts_out):
                dst[...] = src[...].astype(BF16)
        _prompt_tile(step // n_tiles, step % n_tiles, x_ref, mod_ref, prm, win_ref, sguw_ref, sgub_ref,
                     poolw_ref, wbr_ref, wout_ref, fg_ref, xo_ref, convo_ref, poolo_ref, vo_ref, zbuf, cbuf,
                     layer=layer, final=final)

    @pl.when(step == prompt_steps)
    def _():
        _decode_rows(xs_ref, mods_ref, sconv_ref, spool_hbm, prm, sguw_ref, win_ref, poolw_ref, wbr_ref, wout_ref,
                     fg_ref,
                     xso_ref, convso_ref, vso_ref, poolso_hbm, state, cx_stage, sem, layer=layer, final=final)


def _layer_call(layer, final, x, xs, mod_p, mod_s, sconv, spool, rows_p, sgub, fg, weights, sguw, poolw, next_f32,
                states):
    n, seq, d = x.shape
    rows = xs.shape[0]
    w = d // 2
    tm = SEQ_TILE
    n_tiles = seq // tm
    prompt_steps = n * n_tiles
    resident = dict(pipeline_mode=pl.Buffered(1))
    whole = lambda a: pl.BlockSpec(a.shape, lambda s: (0,) * a.ndim, **resident)
    of_layer = lambda a: pl.BlockSpec((None,) + a.shape[1:], lambda s: (layer,) + (0,) * (a.ndim - 1), **resident)
    win, wbr, wout = weights

    ptile = lambda s: jnp.minimum(s, prompt_steps - 1)
    x_tile = lambda s: (ptile(s) // n_tiles, ptile(s) % n_tiles, 0)
    batch_row = lambda s: (layer, ptile(s) // n_tiles, 0, 0)
    hbm = pl.BlockSpec(memory_space=pl.ANY)

    in_specs = [
        pl.BlockSpec((1, tm, d), x_tile),
        whole(mod_p), whole(win), of_layer(sguw), whole(sgub), of_layer(poolw), whole(wbr), whole(wout), whole(fg),
        whole(xs), of_layer(mod_s), of_layer(sconv), hbm,
        *[whole(a) for a in rows_p],
    ]
    out_specs = [
        pl.BlockSpec((1, tm, d), x_tile),
        pl.BlockSpec((None, 1, CONV_W - 1, w), batch_row),
        pl.BlockSpec((None, POOL_BUF, n, w), lambda s: (layer, 0, 0, 0)),
        pl.BlockSpec((None, 1, CHUNK, w), batch_row),
        pl.BlockSpec((rows, d), lambda s: (0, 0)),
        pl.BlockSpec((None, rows, (CONV_W - 1) * w), lambda s: (layer, 0, 0)),
        pl.BlockSpec((None, rows, w), lambda s: (layer, 0, 0)),
        hbm,
    ]
    conv_p, pool_p, v_p, conv_s, v_s, pool_s = (jax.ShapeDtypeStruct(a.shape, F32) for a in states)
    out_shape = [jax.ShapeDtypeStruct((n, seq, d), F32), conv_p, pool_p, v_p,
                 jax.ShapeDtypeStruct((rows, d), F32), conv_s, v_s, pool_s]
    state_outputs = (1, 2, 3, 5, 6, 7)
    assert len(state_outputs) == len(states) == N_STATE_OUTPUTS
    assert len(rows_p) == N_ROW_PARAMS
    args = [x, mod_p, win, sguw, sgub, poolw, wbr, wout, fg, xs, mod_s, sconv, spool, *rows_p]
    aliases = {len(args) + i: o for i, o in enumerate(state_outputs)}
    in_specs += [hbm] * len(states)
    args += list(states)
    if not final:
        for src in next_f32:
            nrows, cols = src.shape[1:]
            assert nrows % prompt_steps == 0 and (nrows // prompt_steps) % BF16_SUBLANES == 0
            blk = nrows // prompt_steps
            in_specs.append(pl.BlockSpec((None, blk, cols), lambda s: (layer + 1, ptile(s), 0)))
            out_specs.append(pl.BlockSpec((blk, cols), lambda s: (ptile(s), 0)))
            out_shape.append(jax.ShapeDtypeStruct((nrows, cols), BF16))
            args.append(src)
    scratch_shapes = [
        pltpu.VMEM((SUBLANES + tm, w), F32),
        pltpu.VMEM((2 * SUBLANES + tm, w), F32),
        pltpu.VMEM((POOL_BUF, rows, w), F32),
        pltpu.VMEM((rows, w), F32),
        pltpu.SemaphoreType.DMA((3,)),
    ]
    return pl.pallas_call(
        functools.partial(_layer_kernel, layer=layer, final=final, n_tiles=n_tiles, prompt_steps=prompt_steps),
        grid=(prompt_steps + 1,),
        in_specs=in_specs,
        out_specs=out_specs,
        out_shape=out_shape,
        scratch_shapes=scratch_shapes,
        input_output_aliases=aliases,
        compiler_params=pltpu.CompilerParams(
            dimension_semantics=("arbitrary",),
            vmem_limit_bytes=VMEM_LIMIT_BYTES),
        name=f"layer{layer}",
    )(*args)


def kernel(x_prompt, x_sample, c_prompt, c_sample, state_conv, state_pool, w_ada, b_ada, norm_g, w_in, conv_w,
           conv_b, lnv_g, lnv_b, sgu_w, sgu_b, pool_w, pool_b, pool_scale, w_branch, w_out, final_g):
    n, seq, d = x_prompt.shape
    rows = x_sample.shape[0]
    depth = w_in.shape[0]
    w = d // 2
    hd = w // SGU_HEADS
    assert x_sample.shape[1] == 1 and seq % SEQ_TILE == 0 and SEQ_TILE % CHUNK == 0

    next_f32 = (w_in, w_branch.reshape(depth, N_BRANCH * w, d), w_out)
    state_shapes = ((depth, n, CONV_W - 1, w), (depth, POOL_BUF, n, w), (depth, n, CHUNK, w),
                    (depth, rows, (CONV_W - 1) * w), (depth, rows, w), (depth, POOL_BUF, rows, w))
    mod_p, mod_s, *made = _mod_call(c_prompt, c_sample, w_ada, b_ada, next_f32, state_shapes)
    weights, states = made[:len(next_f32)], tuple(made[len(next_f32):])

    rows_p = (norm_g, conv_b, lnv_g, lnv_b, pool_b, pool_scale, jnp.transpose(conv_w, (1, 0, 2)), sgu_b)
    sgub_full = jnp.repeat(jnp.swapaxes(sgu_b, 1, 2), hd, axis=2)
    fg = final_g.reshape(1, d)
    hist_major = lambda a: jnp.transpose(a, (0, 2, 1, 3))
    sconv = state_conv.reshape(depth, rows, (CONV_W - 1) * w)
    spool = hist_major(state_pool)

    xp, xs = x_prompt, x_sample.reshape(rows, d)
    for l in range(depth):
        xp, conv_p, pool_p, v_p, xs, conv_s, v_s, pool_s, *next_weights = _layer_call(
            l, l == depth - 1, xp, xs, mod_p, mod_s, sconv, spool, rows_p, sgub_full, fg, tuple(weights), sgu_w,
            pool_w, next_f32, states)
        states = (conv_p, pool_p, v_p, conv_s, v_s, pool_s)
        weights = next_weights

    return (xp, xs.reshape(rows, 1, d), conv_p, conv_s.reshape(depth, rows, CONV_W - 1, w), hist_major(pool_p),
            hist_major(pool_s), v_p, v_s.reshape(depth, rows, 1, w))
```

```python
import functools
import math

import jax
import jax.numpy as jnp
from jax import lax
from jax.experimental import pallas as pl
from jax.experimental.pallas import tpu as pltpu

F32 = jnp.float32
BF16 = jnp.bfloat16

CONV_W = 3
CHUNK = 128
SGU_HEADS = 8
POOL_WINDOWS = (2, 4, 8, 16)
POOL_BUF = max(POOL_WINDOWS) - 1
PAST_LEN = 16384
EPS = 1e-6

N_BRANCH = 3
N_ROW_PARAMS = 8
N_STATE_OUTPUTS = 6
SUBLANES = 8
BF16_SUBLANES = 16
SEQ_TILE = 512
HEADS_PER_DOT = 2
VMEM_LIMIT_BYTES = 58 * 1024 * 1024


def _sigmoid(x):
    return 0.5 * (1.0 + jnp.tanh(0.5 * x))


def _silu(x):
    return x * _sigmoid(x)


def _gelu(x):
    c = math.sqrt(2.0 / math.pi)
    return 0.5 * x * (1.0 + jnp.tanh(c * (x + 0.044715 * (x * x * x))))


def _shift_rows(x, k):
    return pltpu.roll(x, k, axis=0)


def _dot(a, b):
    return jnp.dot(a, b, preferred_element_type=F32)


def _modulated_norm(x, norm_g, shift, scale):
    ms = jnp.mean(x * x, axis=-1, keepdims=True)
    return (x * lax.rsqrt(ms + EPS)) * (norm_g * (1.0 + scale)) + shift


def _layer_rows(prm, layer):
    ng, convb, lng, lnb, poolb, pools, convw, _ = prm
    row = lambda ref: ref[layer:layer + 1, :]
    out = dict(norm_g=row(ng), conv_b=row(convb), lnv_g=row(lng), lnv_b=row(lnb), pool_b=row(poolb),
               pool_scale=row(pools))
    for k in range(CONV_W):
        out[f"conv_w{k}"] = convw[k, layer:layer + 1, :]
    return out


def _spread_heads(per_head, w):
    hd = w // len(per_head)
    lane_head = lax.broadcasted_iota(jnp.int32, (1, w), 1) // hd
    out = jnp.zeros((1, w), F32)
    for h, val in enumerate(per_head):
        out = jnp.where(lane_head == h, val, out)
    return out


def _layernorm(x, g, b):
    mu = jnp.mean(x, axis=-1, keepdims=True)
    d = x - mu
    var = jnp.mean(d * d, axis=-1, keepdims=True)
    return d * lax.rsqrt(var + EPS) * g + b


def _mod_kernel(cp_ref, cs_ref, w_ref, b_ref, *rest, n_cast):
    casts_in, (op_ref, os_ref), rest = rest[:n_cast], rest[n_cast:n_cast + 2], rest[n_cast + 2:]
    casts_out, state_refs = rest[:n_cast], rest[n_cast:]
    for ref in state_refs:
        ref[...] = jnp.zeros(ref.shape, ref.dtype)
    n = cp_ref.shape[0]
    c = jnp.concatenate([cp_ref[...], cp_ref[...], cs_ref[...]], axis=0)
    m = _dot(_silu(c).astype(BF16), w_ref[0].astype(BF16)) + b_ref[0]
    op_ref[0] = m[0:n]
    os_ref[0] = m[2 * n:]
    for src, dst in zip(casts_in, casts_out):
        dst[...] = src[...].astype(BF16)


def _mod_call(c_prompt, c_sample, w_ada, b_ada, first_f32, state_shapes):
    depth, d, d3 = w_ada.shape
    n, rows = c_prompt.shape[0], c_sample.shape[0]
    assert n == SUBLANES
    n_col = 2
    col_block = d3 // n_col
    steps = depth * n_col
    in_specs = [
        pl.BlockSpec((n, d), lambda l, j: (0, 0)),
        pl.BlockSpec((rows, d), lambda l, j: (0, 0)),
        pl.BlockSpec((1, d, col_block), lambda l, j: (l, 0, j)),
        pl.BlockSpec((1, 1, col_block), lambda l, j: (l, 0, j)),
    ]
    out_specs = [
        pl.BlockSpec((1, n, col_block), lambda l, j: (l, 0, j)),
        pl.BlockSpec((1, rows, col_block), lambda l, j: (l, 0, j)),
    ]
    out_shape = [
        jax.ShapeDtypeStruct((depth, n, d3), F32),
        jax.ShapeDtypeStruct((depth, rows, d3), F32),
    ]
    for src in first_f32:
        nrows, cols = src.shape[1:]
        assert nrows % steps == 0 and (nrows // steps) % BF16_SUBLANES == 0
        blk = nrows // steps
        in_specs.append(pl.BlockSpec((None, blk, cols), lambda l, j: (0, l * n_col + j, 0)))
        out_specs.append(pl.BlockSpec((blk, cols), lambda l, j: (l * n_col + j, 0)))
        out_shape.append(jax.ShapeDtypeStruct((nrows, cols), BF16))
    for shape in state_shapes:
        assert shape[0] == depth
        out_specs.append(pl.BlockSpec((1,) + shape[1:], lambda l, j, nd=len(shape): (l,) + (0,) * (nd - 1)))
        out_shape.append(jax.ShapeDtypeStruct(shape, F32))
    return pl.pallas_call(
        functools.partial(_mod_kernel, n_cast=len(first_f32)),
        grid=(depth, n_col),
        in_specs=in_specs,
        out_specs=out_specs,
        out_shape=out_shape,
        compiler_params=pltpu.CompilerParams(
            dimension_semantics=("arbitrary", "arbitrary"),
            vmem_limit_bytes=VMEM_LIMIT_BYTES),
        name="adaln_mod",
    )(c_prompt, c_sample, w_ada, b_ada.reshape(depth, 1, d3), *first_f32)


def _prompt_tile(b, t, x_ref, mod_ref, prm, win_ref, sguw_ref, sgub_ref, poolw_ref, wbr_ref, wout_ref, fg_ref,
                 xo_ref, convo_ref, poolo_ref, vo_ref, zbuf, cbuf, *, layer, final):
    tm, d = x_ref.shape[1], x_ref.shape[2]
    w = d // 2
    p = _layer_rows(prm, layer)
    zoff = SUBLANES
    coff = 2 * SUBLANES

    @pl.when(t == 0)
    def _():
        zbuf[0:zoff, :] = jnp.zeros((zoff, w), F32)
        cbuf[0:coff, :] = jnp.zeros((coff, w), F32)

    x = x_ref[0]
    mod = mod_ref[layer, pl.ds(b, 1), :]
    shift, scale, gate = mod[:, 0:d], mod[:, d:2 * d], mod[:, 2 * d:3 * d]
    hb = _modulated_norm(x, p["norm_g"], shift, scale).astype(BF16)

    pc = _dot(hb, win_ref[:, 7 * w:9 * w])
    pb = _dot(hb, win_ref[:, 4 * w:7 * w])
    pa = _dot(hb, win_ref[:, 0:4 * w])
    pg = _dot(hb, win_ref[:, 9 * w:9 * w + 3 * d])

    c_x = pc[:, 0:w]
    cbuf[coff:coff + tm, :] = c_x
    gd = w // len(POOL_WINDOWS)
    pos1 = lax.broadcasted_iota(jnp.int32, (tm, gd), 0) + (t * tm + 1)
    yc_groups = []
    for g, win in enumerate(POOL_WINDOWS):
        cols = slice(g * gd, (g + 1) * gd)
        s = cbuf[:, cols]
        k = 1
        while k < win:
            s = s + _shift_rows(s, k)
            k *= 2
        s = s[coff:, :]
        cnt = jnp.minimum(pos1, win).astype(F32)
        pm = s / cnt - c_x[:, cols]
        yc_groups.append(_dot(pm.astype(BF16), poolw_ref[g].astype(BF16)))
    y_c = jnp.concatenate(yc_groups, axis=1)
    y_c = (y_c + p["pool_b"]) * p["pool_scale"] * _silu(pc[:, w:2 * w])

    u = _gelu(pb[:, 0:w])
    v = _layernorm(_gelu(pb[:, w:2 * w]), p["lnv_g"], p["lnv_b"])
    vb = v.astype(BF16)
    causal = (lax.broadcasted_iota(jnp.int32, (CHUNK, CHUNK), 1)
              <= lax.broadcasted_iota(jnp.int32, (CHUNK, CHUNK), 0))
    hd = w // SGU_HEADS
    gcols = HEADS_PER_DOT * hd
    lane_head = lax.broadcasted_iota(jnp.int32, (CHUNK, gcols), 1) // hd
    bias = sgub_ref[layer]
    wgs = [jnp.concatenate([jnp.where(causal, sguw_ref[g * HEADS_PER_DOT + hh], 0.0).astype(BF16)
                            for hh in range(HEADS_PER_DOT)], axis=1)
           for g in range(SGU_HEADS // HEADS_PER_DOT)]
    mixed_rows = []
    for c in range(tm // CHUNK):
        vc = vb[c * CHUNK:(c + 1) * CHUNK, :]
        outs = []
        for g, wg in enumerate(wgs):
            vg = vc[:, g * gcols:(g + 1) * gcols]
            rhs = jnp.concatenate(
                [jnp.where(lane_head == hh, vg, jnp.zeros((), BF16)) for hh in range(HEADS_PER_DOT)], axis=0)
            outs.append(_dot(wg, rhs))
        mixed_rows.append(jnp.concatenate(outs, axis=1) + bias)
    mixed = jnp.concatenate(mixed_rows, axis=0)
    y_b = u * mixed * _silu(pb[:, 2 * w:3 * w])

    z = pa[:, w:2 * w] * pa[:, 2 * w:3 * w]
    zbuf[zoff:zoff + tm, :] = z
    zfull = zbuf[...]
    conv = p["conv_b"] + (_shift_rows(zfull, 2)[zoff:, :] * p["conv_w0"]
                          + _shift_rows(zfull, 1)[zoff:, :] * p["conv_w1"]
                          + z * p["conv_w2"])
    y_a = pa[:, 0:w] * conv * _silu(pa[:, 3 * w:4 * w])
    zbuf[0:zoff, :] = zbuf[tm:tm + zoff, :]

    convo_ref[0] = z[tm - (CONV_W - 1):tm, :]
    for j in range(POOL_BUF):
        r = coff + tm - POOL_BUF + j
        poolo_ref[j, pl.ds(b, 1), :] = cbuf[r:r + 1, :]
    vo_ref[0] = v[tm - CHUNK:tm, :]
    cbuf[0:coff, :] = cbuf[tm:tm + coff, :]

    da = _dot(y_a.astype(BF16), wbr_ref[0:w, :])
    db = _dot(y_b.astype(BF16), wbr_ref[w:2 * w, :])
    dc = _dot(y_c.astype(BF16), wbr_ref[2 * w:3 * w, :])
    merged = _sigmoid(pg[:, 0:d]) * da + _sigmoid(pg[:, d:2 * d]) * db + _sigmoid(pg[:, 2 * d:3 * d]) * dc
    x_new = x + gate * _dot(merged.astype(BF16), wout_ref[...])
    if final:
        ms = jnp.mean(x_new * x_new, axis=-1, keepdims=True)
        x_new = x_new * lax.rsqrt(ms + EPS) * fg_ref[...]
    xo_ref[0] = x_new


def _decode_rows(xs_ref, mods_ref, sconv_ref, spool_hbm, prm, sguw_ref, win_ref, poolw_ref, wbr_ref, wout_ref, fg_ref,
                 xso_ref, convso_ref, vso_ref, poolso_hbm, state, cx_stage, sem, *, layer, final):
    d = xs_ref.shape[-1]
    w = d // 2
    p = _layer_rows(prm, layer)
    gd = w // len(POOL_WINDOWS)

    load = pltpu.make_async_copy(spool_hbm.at[layer], state, sem.at[0])
    load.start()

    x = xs_ref[...] if len(xs_ref.shape) == 2 else xs_ref[:, 0, :]
    mod = mods_ref[...]
    shift, scale, gate = mod[:, 0:d], mod[:, d:2 * d], mod[:, 2 * d:3 * d]
    hb = _modulated_norm(x, p["norm_g"], shift, scale).astype(BF16)

    pa = _dot(hb, win_ref[:, 0:4 * w])
    z = pa[:, w:2 * w] * pa[:, 2 * w:3 * w]
    prev0, prev1 = sconv_ref[:, 0, :], sconv_ref[:, 1, :]
    conv = p["conv_b"] + (prev0 * p["conv_w0"] + prev1 * p["conv_w1"] + z * p["conv_w2"])
    y_a = pa[:, 0:w] * conv * _silu(pa[:, 3 * w:4 * w])
    convso_ref[:, 0, :] = prev1
    convso_ref[:, 1, :] = z

    pb = _dot(hb, win_ref[:, 4 * w:7 * w])
    u = _gelu(pb[:, 0:w])
    v = _layernorm(_gelu(pb[:, w:2 * w]), p["lnv_g"], p["lnv_b"])
    vso_ref[:, 0, :] = v
    sgu_w0 = _spread_heads([sguw_ref[h, 0:1, 0:1] for h in range(SGU_HEADS)], w)
    sgu_b0 = _spread_heads([prm[-1][layer, h:h + 1, 0:1] for h in range(SGU_HEADS)], w)
    y_b = u * (v * sgu_w0 + sgu_b0) * _silu(pb[:, 2 * w:3 * w])

    pc = _dot(hb, win_ref[:, 7 * w:9 * w])
    c_x = pc[:, 0:w]
    cx_stage[...] = c_x
    newest = pltpu.make_async_copy(cx_stage, poolso_hbm.at[layer, POOL_BUF - 1], sem.at[1])
    newest.start()
    load.wait()
    keep = pltpu.make_async_copy(state.at[pl.ds(1, POOL_BUF - 1)], poolso_hbm.at[layer, pl.ds(0, POOL_BUF - 1)],
                                 sem.at[2])
    keep.start()
    yc_groups = []
    for g, win in enumerate(POOL_WINDOWS):
        s = c_x[:, g * gd:(g + 1) * gd]
        for j in range(POOL_BUF - (win - 1), POOL_BUF):
            s = s + state[j, :, g * gd:(g + 1) * gd]
        pm = s / float(min(win, PAST_LEN + 1)) - c_x[:, g * gd:(g + 1) * gd]
        yc_groups.append(_dot(pm.astype(BF16), poolw_ref[g].astype(BF16)))
    y_c = jnp.concatenate(yc_groups, axis=1)
    y_c = (y_c + p["pool_b"]) * p["pool_scale"] * _silu(pc[:, w:2 * w])

    pg = _dot(hb, win_ref[:, 9 * w:9 * w + 3 * d])
    merged = (_sigmoid(pg[:, 0:d]) * _dot(y_a.astype(BF16), wbr_ref[0:w, :])
              + _sigmoid(pg[:, d:2 * d]) * _dot(y_b.astype(BF16), wbr_ref[w:2 * w, :])
              + _sigmoid(pg[:, 2 * d:3 * d]) * _dot(y_c.astype(BF16), wbr_ref[2 * w:3 * w, :]))
    x_new = x + gate * _dot(merged.astype(BF16), wout_ref[...])
    if final:
        ms = jnp.mean(x_new * x_new, axis=-1, keepdims=True)
        x_new = x_new * lax.rsqrt(ms + EPS) * fg_ref[...]
    if len(xso_ref.shape) == 2:
        xso_ref[...] = x_new
    else:
        xso_ref[:, 0, :] = x_new
    newest.wait()
    keep.wait()


def _layer_kernel(x_ref, mod_ref, win_ref, sguw_ref, sgub_ref, poolw_ref, wbr_ref, wout_ref, fg_ref,
                  xs_ref, mods_ref, sconv_ref, spool_hbm,
                  *rest, layer, final, n_tiles, prompt_steps):
    prm, rest = rest[:N_ROW_PARAMS], rest[N_ROW_PARAMS:]
    rest = rest[N_STATE_OUTPUTS:]
    if final:
        outs, scratch = rest[:8], rest[8:]
    else:
        casts_in, outs, casts_out, scratch = rest[:3], rest[3:11], rest[11:14], rest[14:]
    xo_ref, convo_ref, poolo_ref, vo_ref, xso_ref, convso_ref, vso_ref, poolso_hbm = outs
    zbuf, cbuf, state, cx_stage, sem = scratch
    step = pl.program_id(0)

    @pl.when(step < prompt_steps)
    def _():
        if not final:
            for src, dst in zip(casts_in, casts_out):
                dst[...] = src[...].astype(BF16)
        _prompt_tile(step // n_tiles, step % n_tiles, x_ref, mod_ref, prm, win_ref, sguw_ref, sgub_ref,
                     poolw_ref, wbr_ref, wout_ref, fg_ref, xo_ref, convo_ref, poolo_ref, vo_ref, zbuf, cbuf,
                     layer=layer, final=final)

    @pl.when(step == prompt_steps)
    def _():
        _decode_rows(xs_ref, mods_ref, sconv_ref, spool_hbm, prm, sguw_ref, win_ref, poolw_ref, wbr_ref, wout_ref,
                     fg_ref,
                     xso_ref, convso_ref, vso_ref, poolso_hbm, state, cx_stage, sem, layer=layer, final=final)


def _layer_call(layer, final, x, xs, mod_p, mod_s, sconv, spool, rows_p, sgub, fg, weights, sguw, poolw, next_f32,
                states):
    n, seq, d = x.shape
    rows = xs.shape[0]
    w = d // 2
    tm = SEQ_TILE
    n_tiles = seq // tm
    prompt_steps = n * n_tiles
    xs_out = jax.ShapeDtypeStruct((rows, 1, d) if final else (rows, d), F32)
    resident = dict(pipeline_mode=pl.Buffered(1))
    whole = lambda a: pl.BlockSpec(a.shape, lambda s: (0,) * a.ndim, **resident)
    of_layer = lambda a: pl.BlockSpec((None,) + a.shape[1:], lambda s: (layer,) + (0,) * (a.ndim - 1), **resident)
    win, wbr, wout = weights

    ptile = lambda s: jnp.minimum(s, prompt_steps - 1)
    x_tile = lambda s: (ptile(s) // n_tiles, ptile(s) % n_tiles, 0)
    batch_row = lambda s: (layer, ptile(s) // n_tiles, 0, 0)
    hbm = pl.BlockSpec(memory_space=pl.ANY)

    in_specs = [
        pl.BlockSpec((1, tm, d), x_tile),
        whole(mod_p), whole(win), of_layer(sguw), whole(sgub), of_layer(poolw), whole(wbr), whole(wout), whole(fg),
        whole(xs), of_layer(mod_s), of_layer(sconv), hbm,
        *[whole(a) for a in rows_p],
    ]
    out_specs = [
        pl.BlockSpec((1, tm, d), x_tile),
        pl.BlockSpec((None, 1, CONV_W - 1, w), batch_row),
        pl.BlockSpec((None, POOL_BUF, n, w), lambda s: (layer, 0, 0, 0)),
        pl.BlockSpec((None, 1, CHUNK, w), batch_row),
        pl.BlockSpec(xs_out.shape, lambda s: (0,) * len(xs_out.shape)),
        pl.BlockSpec((None, rows, CONV_W - 1, w), lambda s: (layer, 0, 0, 0)),
        pl.BlockSpec((None, rows, 1, w), lambda s: (layer, 0, 0, 0)),
        hbm,
    ]
    conv_p, pool_p, v_p, conv_s, v_s, pool_s = (jax.ShapeDtypeStruct(a.shape, F32) for a in states)
    out_shape = [jax.ShapeDtypeStruct((n, seq, d), F32), conv_p, pool_p, v_p,
                 xs_out, conv_s, v_s, pool_s]
    state_outputs = (1, 2, 3, 5, 6, 7)
    assert len(state_outputs) == len(states) == N_STATE_OUTPUTS
    assert len(rows_p) == N_ROW_PARAMS
    args = [x, mod_p, win, sguw, sgub, poolw, wbr, wout, fg, xs, mod_s, sconv, spool, *rows_p]
    aliases = {len(args) + i: o for i, o in enumerate(state_outputs)}
    in_specs += [hbm] * len(states)
    args += list(states)
    if not final:
        for src in next_f32:
            nrows, cols = src.shape[1:]
            assert nrows % prompt_steps == 0 and (nrows // prompt_steps) % BF16_SUBLANES == 0
            blk = nrows // prompt_steps
            in_specs.append(pl.BlockSpec((None, blk, cols), lambda s: (layer + 1, ptile(s), 0)))
            out_specs.append(pl.BlockSpec((blk, cols), lambda s: (ptile(s), 0)))
            out_shape.append(jax.ShapeDtypeStruct((nrows, cols), BF16))
            args.append(src)
    scratch_shapes = [
        pltpu.VMEM((SUBLANES + tm, w), F32),
        pltpu.VMEM((2 * SUBLANES + tm, w), F32),
        pltpu.VMEM((POOL_BUF, rows, w), F32),
        pltpu.VMEM((rows, w), F32),
        pltpu.SemaphoreType.DMA((3,)),
    ]
    return pl.pallas_call(
        functools.partial(_layer_kernel, layer=layer, final=final, n_tiles=n_tiles, prompt_steps=prompt_steps),
        grid=(prompt_steps + 1,),
        in_specs=in_specs,
        out_specs=out_specs,
        out_shape=out_shape,
        scratch_shapes=scratch_shapes,
        input_output_aliases=aliases,
        compiler_params=pltpu.CompilerParams(
            dimension_semantics=("arbitrary",),
            vmem_limit_bytes=VMEM_LIMIT_BYTES),
        name=f"layer{layer}",
    )(*args)


def kernel(x_prompt, x_sample, c_prompt, c_sample, state_conv, state_pool, w_ada, b_ada, norm_g, w_in, conv_w,
           conv_b, lnv_g, lnv_b, sgu_w, sgu_b, pool_w, pool_b, pool_scale, w_branch, w_out, final_g):
    n, seq, d = x_prompt.shape
    rows = x_sample.shape[0]
    depth = w_in.shape[0]
    w = d // 2
    hd = w // SGU_HEADS
    assert x_sample.shape[1] == 1 and seq % SEQ_TILE == 0 and SEQ_TILE % CHUNK == 0

    next_f32 = (w_in, w_branch.reshape(depth, N_BRANCH * w, d), w_out)
    state_shapes = ((depth, n, CONV_W - 1, w), (depth, POOL_BUF, n, w), (depth, n, CHUNK, w),
                    (depth, rows, CONV_W - 1, w), (depth, rows, 1, w), (depth, POOL_BUF, rows, w))
    mod_p, mod_s, *made = _mod_call(c_prompt, c_sample, w_ada, b_ada, next_f32, state_shapes)
    weights, states = made[:len(next_f32)], tuple(made[len(next_f32):])

    rows_p = (norm_g, conv_b, lnv_g, lnv_b, pool_b, pool_scale, jnp.transpose(conv_w, (1, 0, 2)), sgu_b)
    sgub_full = jnp.repeat(jnp.swapaxes(sgu_b, 1, 2), hd, axis=2)
    fg = final_g.reshape(1, d)
    hist_major = lambda a: jnp.transpose(a, (0, 2, 1, 3))
    spool = hist_major(state_pool)

    xp, xs = x_prompt, x_sample
    for l in range(depth):
        xp, conv_p, pool_p, v_p, xs, conv_s, v_s, pool_s, *next_weights = _layer_call(
            l, l == depth - 1, xp, xs, mod_p, mod_s, state_conv, spool, rows_p, sgub_full, fg, tuple(weights), sgu_w,
            pool_w, next_f32, states)
        states = (conv_p, pool_p, v_p, conv_s, v_s, pool_s)
        weights = next_weights

    return xp, xs, conv_p, conv_s, hist_major(pool_p), hist_major(pool_s), v_p, v_s
```

```python
import functools
import math

import jax
import jax.numpy as jnp
from jax import lax
from jax.experimental import pallas as pl
from jax.experimental.pallas import tpu as pltpu

F32 = jnp.float32
BF16 = jnp.bfloat16

CONV_W = 3
CHUNK = 128
SGU_HEADS = 8
POOL_WINDOWS = (2, 4, 8, 16)
POOL_BUF = max(POOL_WINDOWS) - 1
PAST_LEN = 16384
EPS = 1e-6

N_BRANCH = 3
N_WEIGHT_COPIES = 6
N_ROW_PARAMS = 8
N_STATE_OUTPUTS = 6
SUBLANES = 8
BF16_SUBLANES = 16
SEQ_TILE = 512
HEADS_PER_DOT = 2
VMEM_LIMIT_BYTES = 58 * 1024 * 1024


def _sigmoid(x):
    return 0.5 * (1.0 + jnp.tanh(0.5 * x))


def _silu(x):
    return x * _sigmoid(x)


def _gelu(x):
    c = math.sqrt(2.0 / math.pi)
    return 0.5 * x * (1.0 + jnp.tanh(c * (x + 0.044715 * (x * x * x))))


def _shift_rows(x, k):
    return pltpu.roll(x, k, axis=0)


def _dot(a, b):
    return jnp.dot(a, b, preferred_element_type=F32)


def _modulated_norm(x, norm_g, shift, scale):
    ms = jnp.mean(x * x, axis=-1, keepdims=True)
    return (x * lax.rsqrt(ms + EPS)) * (norm_g * (1.0 + scale)) + shift


def _layer_rows(prm, layer):
    ng, convb, lng, lnb, poolb, pools, convw, _ = prm
    row = lambda ref: ref[layer:layer + 1, :]
    out = dict(norm_g=row(ng), conv_b=row(convb), lnv_g=row(lng), lnv_b=row(lnb), pool_b=row(poolb),
               pool_scale=row(pools))
    for k in range(CONV_W):
        out[f"conv_w{k}"] = convw[k, layer:layer + 1, :]
    return out


def _spread_heads(per_head, w):
    hd = w // len(per_head)
    lane_head = lax.broadcasted_iota(jnp.int32, (1, w), 1) // hd
    out = jnp.zeros((1, w), F32)
    for h, val in enumerate(per_head):
        out = jnp.where(lane_head == h, val, out)
    return out


def _layernorm(x, g, b):
    mu = jnp.mean(x, axis=-1, keepdims=True)
    d = x - mu
    var = jnp.mean(d * d, axis=-1, keepdims=True)
    return d * lax.rsqrt(var + EPS) * g + b


def _mod_kernel(cp_ref, cs_ref, w_ref, b_ref, *rest, n_cast):
    casts_in, (op_ref, os_ref), rest = rest[:n_cast], rest[n_cast:n_cast + 2], rest[n_cast + 2:]
    casts_out, state_refs = rest[:n_cast], rest[n_cast:]
    for ref in state_refs:
        ref[...] = jnp.zeros(ref.shape, ref.dtype)
    n = cp_ref.shape[0]
    c = jnp.concatenate([cp_ref[...], cp_ref[...], cs_ref[...]], axis=0)
    m = _dot(_silu(c).astype(BF16), w_ref[0].astype(BF16)) + b_ref[0]
    op_ref[0] = m[0:n]
    os_ref[0] = m[2 * n:]
    for src, dst in zip(casts_in, casts_out):
        dst[...] = src[...].astype(BF16)


def _mod_call(c_prompt, c_sample, w_ada, b_ada, first_f32, state_shapes):
    depth, d, d3 = w_ada.shape
    n, rows = c_prompt.shape[0], c_sample.shape[0]
    assert n == SUBLANES
    n_col = 2
    col_block = d3 // n_col
    steps = depth * n_col
    in_specs = [
        pl.BlockSpec((n, d), lambda l, j: (0, 0)),
        pl.BlockSpec((rows, d), lambda l, j: (0, 0)),
        pl.BlockSpec((1, d, col_block), lambda l, j: (l, 0, j)),
        pl.BlockSpec((1, 1, col_block), lambda l, j: (l, 0, j)),
    ]
    out_specs = [
        pl.BlockSpec((1, n, col_block), lambda l, j: (l, 0, j)),
        pl.BlockSpec((1, rows, col_block), lambda l, j: (l, 0, j)),
    ]
    out_shape = [
        jax.ShapeDtypeStruct((depth, n, d3), F32),
        jax.ShapeDtypeStruct((depth, rows, d3), F32),
    ]
    for src in first_f32:
        nrows, cols = src.shape[1:]
        assert nrows % steps == 0 and (nrows // steps) % BF16_SUBLANES == 0
        blk = nrows // steps
        in_specs.append(pl.BlockSpec((None, blk, cols), lambda l, j: (0, l * n_col + j, 0)))
        out_specs.append(pl.BlockSpec((blk, cols), lambda l, j: (l * n_col + j, 0)))
        out_shape.append(jax.ShapeDtypeStruct((nrows, cols), BF16))
    for shape in state_shapes:
        assert shape[0] == depth
        out_specs.append(pl.BlockSpec((1,) + shape[1:], lambda l, j, nd=len(shape): (l,) + (0,) * (nd - 1)))
        out_shape.append(jax.ShapeDtypeStruct(shape, F32))
    return pl.pallas_call(
        functools.partial(_mod_kernel, n_cast=len(first_f32)),
        grid=(depth, n_col),
        in_specs=in_specs,
        out_specs=out_specs,
        out_shape=out_shape,
        compiler_params=pltpu.CompilerParams(
            dimension_semantics=("arbitrary", "arbitrary"),
            vmem_limit_bytes=VMEM_LIMIT_BYTES),
        name="adaln_mod",
    )(c_prompt, c_sample, w_ada, b_ada.reshape(depth, 1, d3), *first_f32)


def _prompt_tile(b, t, x_ref, mod_ref, prm, win_ref, sguw_ref, sgub_ref, poolw_ref, wbr_ref, wout_ref, fg_ref,
                 xo_ref, convo_ref, poolo_ref, vo_ref, zbuf, cbuf, *, layer, final, weight_copies=None):
    tm, d = x_ref.shape[1], x_ref.shape[2]
    w = d // 2

    def arrived(name, ref):
        if weight_copies is not None:
            weight_copies[name].wait()
        return ref

    p = _layer_rows(prm, layer)
    zoff = SUBLANES
    coff = 2 * SUBLANES

    @pl.when(t == 0)
    def _():
        zbuf[0:zoff, :] = jnp.zeros((zoff, w), F32)
        cbuf[0:coff, :] = jnp.zeros((coff, w), F32)

    x = x_ref[0]
    mod = mod_ref[layer, pl.ds(b, 1), :]
    shift, scale, gate = mod[:, 0:d], mod[:, d:2 * d], mod[:, 2 * d:3 * d]
    hb = _modulated_norm(x, p["norm_g"], shift, scale).astype(BF16)

    pc = _dot(hb, arrived("c", win_ref)[:, 7 * w:9 * w])
    pb = _dot(hb, arrived("b", win_ref)[:, 4 * w:7 * w])
    pa = _dot(hb, arrived("a", win_ref)[:, 0:4 * w])
    pg = _dot(hb, arrived("g", win_ref)[:, 9 * w:9 * w + 3 * d])

    c_x = pc[:, 0:w]
    cbuf[coff:coff + tm, :] = c_x
    gd = w // len(POOL_WINDOWS)
    pos1 = lax.broadcasted_iota(jnp.int32, (tm, gd), 0) + (t * tm + 1)
    yc_groups = []
    for g, win in enumerate(POOL_WINDOWS):
        cols = slice(g * gd, (g + 1) * gd)
        s = cbuf[:, cols]
        k = 1
        while k < win:
            s = s + _shift_rows(s, k)
            k *= 2
        s = s[coff:, :]
        cnt = jnp.minimum(pos1, win).astype(F32)
        pm = s / cnt - c_x[:, cols]
        yc_groups.append(_dot(pm.astype(BF16), poolw_ref[g].astype(BF16)))
    y_c = jnp.concatenate(yc_groups, axis=1)
    y_c = (y_c + p["pool_b"]) * p["pool_scale"] * _silu(pc[:, w:2 * w])

    u = _gelu(pb[:, 0:w])
    v = _layernorm(_gelu(pb[:, w:2 * w]), p["lnv_g"], p["lnv_b"])
    vb = v.astype(BF16)
    causal = (lax.broadcasted_iota(jnp.int32, (CHUNK, CHUNK), 1)
              <= lax.broadcasted_iota(jnp.int32, (CHUNK, CHUNK), 0))
    hd = w // SGU_HEADS
    gcols = HEADS_PER_DOT * hd
    lane_head = lax.broadcasted_iota(jnp.int32, (CHUNK, gcols), 1) // hd
    bias = sgub_ref[layer]
    wgs = [jnp.concatenate([jnp.where(causal, sguw_ref[g * HEADS_PER_DOT + hh], 0.0).astype(BF16)
                            for hh in range(HEADS_PER_DOT)], axis=1)
           for g in range(SGU_HEADS // HEADS_PER_DOT)]
    mixed_rows = []
    for c in range(tm // CHUNK):
        vc = vb[c * CHUNK:(c + 1) * CHUNK, :]
        outs = []
        for g, wg in enumerate(wgs):
            vg = vc[:, g * gcols:(g + 1) * gcols]
            rhs = jnp.concatenate(
                [jnp.where(lane_head == hh, vg, jnp.zeros((), BF16)) for hh in range(HEADS_PER_DOT)], axis=0)
            outs.append(_dot(wg, rhs))
        mixed_rows.append(jnp.concatenate(outs, axis=1) + bias)
    mixed = jnp.concatenate(mixed_rows, axis=0)
    y_b = u * mixed * _silu(pb[:, 2 * w:3 * w])

    z = pa[:, w:2 * w] * pa[:, 2 * w:3 * w]
    zbuf[zoff:zoff + tm, :] = z
    zfull = zbuf[...]
    conv = p["conv_b"] + (_shift_rows(zfull, 2)[zoff:, :] * p["conv_w0"]
                          + _shift_rows(zfull, 1)[zoff:, :] * p["conv_w1"]
                          + z * p["conv_w2"])
    y_a = pa[:, 0:w] * conv * _silu(pa[:, 3 * w:4 * w])
    zbuf[0:zoff, :] = zbuf[tm:tm + zoff, :]

    convo_ref[0] = z[tm - (CONV_W - 1):tm, :]
    for j in range(POOL_BUF):
        r = coff + tm - POOL_BUF + j
        poolo_ref[j, pl.ds(b, 1), :] = cbuf[r:r + 1, :]
    vo_ref[0] = v[tm - CHUNK:tm, :]
    cbuf[0:coff, :] = cbuf[tm:tm + coff, :]

    da = _dot(y_a.astype(BF16), arrived("branch", wbr_ref)[0:w, :])
    db = _dot(y_b.astype(BF16), wbr_ref[w:2 * w, :])
    dc = _dot(y_c.astype(BF16), wbr_ref[2 * w:3 * w, :])
    merged = _sigmoid(pg[:, 0:d]) * da + _sigmoid(pg[:, d:2 * d]) * db + _sigmoid(pg[:, 2 * d:3 * d]) * dc
    x_new = x + gate * _dot(merged.astype(BF16), arrived("out", wout_ref)[...])
    if final:
        ms = jnp.mean(x_new * x_new, axis=-1, keepdims=True)
        x_new = x_new * lax.rsqrt(ms + EPS) * fg_ref[...]
    xo_ref[0] = x_new


def _decode_rows(xs_ref, mods_ref, sconv_ref, spool_hbm, prm, sguw_ref, win_ref, poolw_ref, wbr_ref, wout_ref, fg_ref,
                 xso_ref, convso_ref, vso_ref, poolso_hbm, state, cx_stage, sem, *, layer, final):
    d = xs_ref.shape[-1]
    w = d // 2
    p = _layer_rows(prm, layer)
    gd = w // len(POOL_WINDOWS)

    load = pltpu.make_async_copy(spool_hbm.at[layer], state, sem.at[0])
    load.start()

    x = xs_ref[...] if len(xs_ref.shape) == 2 else xs_ref[:, 0, :]
    mod = mods_ref[...]
    shift, scale, gate = mod[:, 0:d], mod[:, d:2 * d], mod[:, 2 * d:3 * d]
    hb = _modulated_norm(x, p["norm_g"], shift, scale).astype(BF16)

    pa = _dot(hb, win_ref[:, 0:4 * w])
    z = pa[:, w:2 * w] * pa[:, 2 * w:3 * w]
    prev0, prev1 = sconv_ref[:, 0, :], sconv_ref[:, 1, :]
    conv = p["conv_b"] + (prev0 * p["conv_w0"] + prev1 * p["conv_w1"] + z * p["conv_w2"])
    y_a = pa[:, 0:w] * conv * _silu(pa[:, 3 * w:4 * w])
    convso_ref[:, 0, :] = prev1
    convso_ref[:, 1, :] = z

    pb = _dot(hb, win_ref[:, 4 * w:7 * w])
    u = _gelu(pb[:, 0:w])
    v = _layernorm(_gelu(pb[:, w:2 * w]), p["lnv_g"], p["lnv_b"])
    vso_ref[:, 0, :] = v
    sgu_w0 = _spread_heads([sguw_ref[h, 0:1, 0:1] for h in range(SGU_HEADS)], w)
    sgu_b0 = _spread_heads([prm[-1][layer, h:h + 1, 0:1] for h in range(SGU_HEADS)], w)
    y_b = u * (v * sgu_w0 + sgu_b0) * _silu(pb[:, 2 * w:3 * w])

    pc = _dot(hb, win_ref[:, 7 * w:9 * w])
    c_x = pc[:, 0:w]
    cx_stage[...] = c_x
    newest = pltpu.make_async_copy(cx_stage, poolso_hbm.at[layer, POOL_BUF - 1], sem.at[1])
    newest.start()
    load.wait()
    keep = pltpu.make_async_copy(state.at[pl.ds(1, POOL_BUF - 1)], poolso_hbm.at[layer, pl.ds(0, POOL_BUF - 1)],
                                 sem.at[2])
    keep.start()
    yc_groups = []
    for g, win in enumerate(POOL_WINDOWS):
        s = c_x[:, g * gd:(g + 1) * gd]
        for j in range(POOL_BUF - (win - 1), POOL_BUF):
            s = s + state[j, :, g * gd:(g + 1) * gd]
        pm = s / float(min(win, PAST_LEN + 1)) - c_x[:, g * gd:(g + 1) * gd]
        yc_groups.append(_dot(pm.astype(BF16), poolw_ref[g].astype(BF16)))
    y_c = jnp.concatenate(yc_groups, axis=1)
    y_c = (y_c + p["pool_b"]) * p["pool_scale"] * _silu(pc[:, w:2 * w])

    pg = _dot(hb, win_ref[:, 9 * w:9 * w + 3 * d])
    merged = (_sigmoid(pg[:, 0:d]) * _dot(y_a.astype(BF16), wbr_ref[0:w, :])
              + _sigmoid(pg[:, d:2 * d]) * _dot(y_b.astype(BF16), wbr_ref[w:2 * w, :])
              + _sigmoid(pg[:, 2 * d:3 * d]) * _dot(y_c.astype(BF16), wbr_ref[2 * w:3 * w, :]))
    x_new = x + gate * _dot(merged.astype(BF16), wout_ref[...])
    if final:
        ms = jnp.mean(x_new * x_new, axis=-1, keepdims=True)
        x_new = x_new * lax.rsqrt(ms + EPS) * fg_ref[...]
    if len(xso_ref.shape) == 2:
        xso_ref[...] = x_new
    else:
        xso_ref[:, 0, :] = x_new
    newest.wait()
    keep.wait()


def _layer_kernel(x_ref, mod_ref, win_hbm, sguw_ref, sgub_ref, poolw_ref, wbr_hbm, wout_hbm, fg_ref,
                  xs_ref, mods_ref, sconv_ref, spool_hbm,
                  *rest, layer, final, n_tiles, prompt_steps):
    prm, rest = rest[:N_ROW_PARAMS], rest[N_ROW_PARAMS:]
    rest = rest[N_STATE_OUTPUTS:]
    if final:
        outs, scratch = rest[:8], rest[8:]
    else:
        casts_in, outs, casts_out, scratch = rest[:3], rest[3:11], rest[11:14], rest[14:]
    xo_ref, convo_ref, poolo_ref, vo_ref, xso_ref, convso_ref, vso_ref, poolso_hbm = outs
    zbuf, cbuf, state, cx_stage, sem, win_ref, wbr_ref, wout_ref, wsem = scratch
    step = pl.program_id(0)
    d = x_ref.shape[2]
    w = d // 2

    def prompt_tile(weight_copies):
        if not final:
            for src, dst in zip(casts_in, casts_out):
                dst[...] = src[...].astype(BF16)
        _prompt_tile(step // n_tiles, step % n_tiles, x_ref, mod_ref, prm, win_ref, sguw_ref, sgub_ref,
                     poolw_ref, wbr_ref, wout_ref, fg_ref, xo_ref, convo_ref, poolo_ref, vo_ref, zbuf, cbuf,
                     layer=layer, final=final, weight_copies=weight_copies)

    @pl.when(step == 0)
    def _():
        win_slices = dict(c=(7 * w, 2 * w), b=(4 * w, 3 * w), a=(0, 4 * w), g=(9 * w, 3 * d))
        copies = {name: pltpu.make_async_copy(win_hbm.at[:, pl.ds(lo, size)], win_ref.at[:, pl.ds(lo, size)],
                                              wsem.at[i])
                  for i, (name, (lo, size)) in enumerate(win_slices.items())}
        copies["branch"] = pltpu.make_async_copy(wbr_hbm, wbr_ref, wsem.at[len(win_slices)])
        copies["out"] = pltpu.make_async_copy(wout_hbm, wout_ref, wsem.at[len(win_slices) + 1])
        for cp in copies.values():
            cp.start()
        prompt_tile(copies)

    @pl.when((step > 0) & (step < prompt_steps))
    def _():
        prompt_tile(None)

    @pl.when(step == prompt_steps)
    def _():
        _decode_rows(xs_ref, mods_ref, sconv_ref, spool_hbm, prm, sguw_ref, win_ref, poolw_ref, wbr_ref, wout_ref,
                     fg_ref,
                     xso_ref, convso_ref, vso_ref, poolso_hbm, state, cx_stage, sem, layer=layer, final=final)


def _layer_call(layer, final, x, xs, mod_p, mod_s, sconv, spool, rows_p, sgub, fg, weights, sguw, poolw, next_f32,
                states):
    n, seq, d = x.shape
    rows = xs.shape[0]
    w = d // 2
    tm = SEQ_TILE
    n_tiles = seq // tm
    prompt_steps = n * n_tiles
    xs_out = jax.ShapeDtypeStruct((rows, 1, d) if final else (rows, d), F32)
    resident = dict(pipeline_mode=pl.Buffered(1))
    whole = lambda a: pl.BlockSpec(a.shape, lambda s: (0,) * a.ndim, **resident)
    of_layer = lambda a: pl.BlockSpec((None,) + a.shape[1:], lambda s: (layer,) + (0,) * (a.ndim - 1), **resident)
    win, wbr, wout = weights

    ptile = lambda s: jnp.minimum(s, prompt_steps - 1)
    x_tile = lambda s: (ptile(s) // n_tiles, ptile(s) % n_tiles, 0)
    batch_row = lambda s: (layer, ptile(s) // n_tiles, 0, 0)
    hbm = pl.BlockSpec(memory_space=pl.ANY)

    in_specs = [
        pl.BlockSpec((1, tm, d), x_tile),
        whole(mod_p), hbm, of_layer(sguw), whole(sgub), of_layer(poolw), hbm, hbm, whole(fg),
        whole(xs), of_layer(mod_s), of_layer(sconv), hbm,
        *[whole(a) for a in rows_p],
    ]
    out_specs = [
        pl.BlockSpec((1, tm, d), x_tile),
        pl.BlockSpec((None, 1, CONV_W - 1, w), batch_row),
        pl.BlockSpec((None, POOL_BUF, n, w), lambda s: (layer, 0, 0, 0)),
        pl.BlockSpec((None, 1, CHUNK, w), batch_row),
        pl.BlockSpec(xs_out.shape, lambda s: (0,) * len(xs_out.shape)),
        pl.BlockSpec((None, rows, CONV_W - 1, w), lambda s: (layer, 0, 0, 0)),
        pl.BlockSpec((None, rows, 1, w), lambda s: (layer, 0, 0, 0)),
        hbm,
    ]
    conv_p, pool_p, v_p, conv_s, v_s, pool_s = (jax.ShapeDtypeStruct(a.shape, F32) for a in states)
    out_shape = [jax.ShapeDtypeStruct((n, seq, d), F32), conv_p, pool_p, v_p,
                 xs_out, conv_s, v_s, pool_s]
    state_outputs = (1, 2, 3, 5, 6, 7)
    assert len(state_outputs) == len(states) == N_STATE_OUTPUTS
    assert len(rows_p) == N_ROW_PARAMS
    args = [x, mod_p, win, sguw, sgub, poolw, wbr, wout, fg, xs, mod_s, sconv, spool, *rows_p]
    aliases = {len(args) + i: o for i, o in enumerate(state_outputs)}
    in_specs += [hbm] * len(states)
    args += list(states)
    if not final:
        for src in next_f32:
            nrows, cols = src.shape[1:]
            assert nrows % prompt_steps == 0 and (nrows // prompt_steps) % BF16_SUBLANES == 0
            blk = nrows // prompt_steps
            in_specs.append(pl.BlockSpec((None, blk, cols), lambda s: (layer + 1, ptile(s), 0)))
            out_specs.append(pl.BlockSpec((blk, cols), lambda s: (ptile(s), 0)))
            out_shape.append(jax.ShapeDtypeStruct((nrows, cols), BF16))
            args.append(src)
    scratch_shapes = [
        pltpu.VMEM((SUBLANES + tm, w), F32),
        pltpu.VMEM((2 * SUBLANES + tm, w), F32),
        pltpu.VMEM((POOL_BUF, rows, w), F32),
        pltpu.VMEM((rows, w), F32),
        pltpu.SemaphoreType.DMA((3,)),
        pltpu.VMEM(win.shape, BF16),
        pltpu.VMEM(wbr.shape, BF16),
        pltpu.VMEM(wout.shape, BF16),
        pltpu.SemaphoreType.DMA((N_WEIGHT_COPIES,)),
    ]
    return pl.pallas_call(
        functools.partial(_layer_kernel, layer=layer, final=final, n_tiles=n_tiles, prompt_steps=prompt_steps),
        grid=(prompt_steps + 1,),
        in_specs=in_specs,
        out_specs=out_specs,
        out_shape=out_shape,
        scratch_shapes=scratch_shapes,
        input_output_aliases=aliases,
        compiler_params=pltpu.CompilerParams(
            dimension_semantics=("arbitrary",),
            vmem_limit_bytes=VMEM_LIMIT_BYTES),
        name=f"layer{layer}",
    )(*args)


def kernel(x_prompt, x_sample, c_prompt, c_sample, state_conv, state_pool, w_ada, b_ada, norm_g, w_in, conv_w,
           conv_b, lnv_g, lnv_b, sgu_w, sgu_b, pool_w, pool_b, pool_scale, w_branch, w_out, final_g):
    n, seq, d = x_prompt.shape
    rows = x_sample.shape[0]
    depth = w_in.shape[0]
    w = d // 2
    hd = w // SGU_HEADS
    assert x_sample.shape[1] == 1 and seq % SEQ_TILE == 0 and SEQ_TILE % CHUNK == 0

    next_f32 = (w_in, w_branch.reshape(depth, N_BRANCH * w, d), w_out)
    state_shapes = ((depth, n, CONV_W - 1, w), (depth, POOL_BUF, n, w), (depth, n, CHUNK, w),
                    (depth, rows, CONV_W - 1, w), (depth, rows, 1, w), (depth, POOL_BUF, rows, w))
    mod_p, mod_s, *made = _mod_call(c_prompt, c_sample, w_ada, b_ada, next_f32, state_shapes)
    weights, states = made[:len(next_f32)], tuple(made[len(next_f32):])

    rows_p = (norm_g, conv_b, lnv_g, lnv_b, pool_b, pool_scale, jnp.transpose(conv_w, (1, 0, 2)), sgu_b)
    sgub_full = jnp.repeat(jnp.swapaxes(sgu_b, 1, 2), hd, axis=2)
    fg = final_g.reshape(1, d)
    hist_major = lambda a: jnp.transpose(a, (0, 2, 1, 3))
    spool = hist_major(state_pool)

    xp, xs = x_prompt, x_sample
    for l in range(depth):
        xp, conv_p, pool_p, v_p, xs, conv_s, v_s, pool_s, *next_weights = _layer_call(
            l, l == depth - 1, xp, xs, mod_p, mod_s, state_conv, spool, rows_p, sgub_full, fg, tuple(weights), sgu_w,
            pool_w, next_f32, states)
        states = (conv_p, pool_p, v_p, conv_s, v_s, pool_s)
        weights = next_weights

    return xp, xs, conv_p, conv_s, hist_major(pool_p), hist_major(pool_s), v_p, v_s
```

```python
import functools
import math

import jax
import jax.numpy as jnp
from jax import lax
from jax.experimental import pallas as pl
from jax.experimental.pallas import tpu as pltpu

F32 = jnp.float32
BF16 = jnp.bfloat16

CONV_W = 3
CHUNK = 128
SGU_HEADS = 8
POOL_WINDOWS = (2, 4, 8, 16)
POOL_BUF = max(POOL_WINDOWS) - 1
PAST_LEN = 16384
EPS = 1e-6

N_BRANCH = 3
N_ROW_PARAMS = 8
N_STATE_OUTPUTS = 6
SUBLANES = 8
BF16_SUBLANES = 16
SEQ_TILE = 512
HEADS_PER_DOT = 2
VMEM_LIMIT_BYTES = 58 * 1024 * 1024


def _sigmoid(x):
    return 0.5 * (1.0 + jnp.tanh(0.5 * x))


def _silu(x):
    return x * _sigmoid(x)


def _gelu(x):
    c = math.sqrt(2.0 / math.pi)
    return 0.5 * x * (1.0 + jnp.tanh(c * (x + 0.044715 * (x * x * x))))


def _shift_rows(x, k):
    return pltpu.roll(x, k, axis=0)


def _dot(a, b):
    return jnp.dot(a, b, preferred_element_type=F32)


def _modulated_norm(x, norm_g, shift, scale):
    ms = jnp.mean(x * x, axis=-1, keepdims=True)
    return (x * lax.rsqrt(ms + EPS)) * (norm_g * (1.0 + scale)) + shift


def _layer_rows(prm, layer):
    ng, convb, lng, lnb, poolb, pools, convw, _ = prm
    row = lambda ref: ref[layer:layer + 1, :]
    out = dict(norm_g=row(ng), conv_b=row(convb), lnv_g=row(lng), lnv_b=row(lnb), pool_b=row(poolb),
               pool_scale=row(pools))
    for k in range(CONV_W):
        out[f"conv_w{k}"] = convw[k, layer:layer + 1, :]
    return out


def _spread_heads(per_head, w):
    hd = w // len(per_head)
    lane_head = lax.broadcasted_iota(jnp.int32, (1, w), 1) // hd
    out = jnp.zeros((1, w), F32)
    for h, val in enumerate(per_head):
        out = jnp.where(lane_head == h, val, out)
    return out


def _layernorm(x, g, b):
    mu = jnp.mean(x, axis=-1, keepdims=True)
    d = x - mu
    var = jnp.mean(d * d, axis=-1, keepdims=True)
    return d * lax.rsqrt(var + EPS) * g + b


def _mod_kernel(cp_ref, cs_ref, w_ref, b_ref, *rest, n_cast):
    casts_in, (op_ref, os_ref), rest = rest[:n_cast], rest[n_cast:n_cast + 2], rest[n_cast + 2:]
    casts_out, state_refs = rest[:n_cast], rest[n_cast:]
    for ref in state_refs:
        ref[...] = jnp.zeros(ref.shape, ref.dtype)
    n = cp_ref.shape[0]
    c = jnp.concatenate([cp_ref[...], cp_ref[...], cs_ref[...]], axis=0)
    m = _dot(_silu(c).astype(BF16), w_ref[0].astype(BF16)) + b_ref[pl.ds(pl.program_id(0), 1), :]
    op_ref[0] = m[0:n]
    os_ref[0] = m[2 * n:]
    for src, dst in zip(casts_in, casts_out):
        dst[...] = src[...].astype(BF16)


def _mod_call(c_prompt, c_sample, w_ada, b_ada, first_f32, state_shapes):
    depth, d, d3 = w_ada.shape
    n, rows = c_prompt.shape[0], c_sample.shape[0]
    assert n == SUBLANES
    n_col = 4
    col_block = d3 // n_col
    steps = depth * n_col
    in_specs = [
        pl.BlockSpec((n, d), lambda l, j: (0, 0)),
        pl.BlockSpec((rows, d), lambda l, j: (0, 0)),
        pl.BlockSpec((1, d, col_block), lambda l, j: (l, 0, j)),
        pl.BlockSpec((depth, col_block), lambda l, j: (0, j)),
    ]
    out_specs = [
        pl.BlockSpec((1, n, col_block), lambda l, j: (l, 0, j)),
        pl.BlockSpec((1, rows, col_block), lambda l, j: (l, 0, j)),
    ]
    out_shape = [
        jax.ShapeDtypeStruct((depth, n, d3), F32),
        jax.ShapeDtypeStruct((depth, rows, d3), F32),
    ]
    for src in first_f32:
        nrows, cols = src.shape[1:]
        assert nrows % steps == 0 and (nrows // steps) % BF16_SUBLANES == 0
        blk = nrows // steps
        in_specs.append(pl.BlockSpec((None, blk, cols), lambda l, j: (0, l * n_col + j, 0)))
        out_specs.append(pl.BlockSpec((blk, cols), lambda l, j: (l * n_col + j, 0)))
        out_shape.append(jax.ShapeDtypeStruct((nrows, cols), BF16))
    for shape in state_shapes:
        assert shape[0] == depth
        out_specs.append(pl.BlockSpec((1,) + shape[1:], lambda l, j, nd=len(shape): (l,) + (0,) * (nd - 1)))
        out_shape.append(jax.ShapeDtypeStruct(shape, F32))
    return pl.pallas_call(
        functools.partial(_mod_kernel, n_cast=len(first_f32)),
        grid=(depth, n_col),
        in_specs=in_specs,
        out_specs=out_specs,
        out_shape=out_shape,
        compiler_params=pltpu.CompilerParams(
            dimension_semantics=("arbitrary", "arbitrary"),
            vmem_limit_bytes=VMEM_LIMIT_BYTES),
        name="adaln_mod",
    )(c_prompt, c_sample, w_ada, b_ada, *first_f32)


def _prompt_tile(b, t, x_ref, mod_ref, prm, win_ref, sguw_ref, sgub_ref, poolw_ref, wbr_ref, wout_ref, fg_ref,
                 xo_ref, convo_ref, poolo_ref, vo_ref, zbuf, cbuf, *, layer, final):
    tm, d = x_ref.shape[1], x_ref.shape[2]
    w = d // 2
    p = _layer_rows(prm, layer)
    zoff = SUBLANES
    coff = 2 * SUBLANES

    @pl.when(t == 0)
    def _():
        zbuf[0:zoff, :] = jnp.zeros((zoff, w), F32)
        cbuf[0:coff, :] = jnp.zeros((coff, w), F32)

    x = x_ref[0]
    mod = mod_ref[layer, pl.ds(b, 1), :]
    shift, scale, gate = mod[:, 0:d], mod[:, d:2 * d], mod[:, 2 * d:3 * d]
    hb = _modulated_norm(x, p["norm_g"], shift, scale).astype(BF16)

    pc = _dot(hb, win_ref[:, 7 * w:9 * w])
    pb = _dot(hb, win_ref[:, 4 * w:7 * w])
    pa = _dot(hb, win_ref[:, 0:4 * w])
    pg = _dot(hb, win_ref[:, 9 * w:9 * w + 3 * d])

    c_x = pc[:, 0:w]
    cbuf[coff:coff + tm, :] = c_x
    gd = w // len(POOL_WINDOWS)
    pos1 = lax.broadcasted_iota(jnp.int32, (tm, gd), 0) + (t * tm + 1)
    yc_groups = []
    for g, win in enumerate(POOL_WINDOWS):
        cols = slice(g * gd, (g + 1) * gd)
        s = cbuf[:, cols]
        k = 1
        while k < win:
            s = s + _shift_rows(s, k)
            k *= 2
        s = s[coff:, :]
        cnt = jnp.minimum(pos1, win).astype(F32)
        pm = s / cnt - c_x[:, cols]
        yc_groups.append(_dot(pm.astype(BF16), poolw_ref[g].astype(BF16)))
    y_c = jnp.concatenate(yc_groups, axis=1)
    y_c = (y_c + p["pool_b"]) * p["pool_scale"] * _silu(pc[:, w:2 * w])

    u = _gelu(pb[:, 0:w])
    v = _layernorm(_gelu(pb[:, w:2 * w]), p["lnv_g"], p["lnv_b"])
    vb = v.astype(BF16)
    causal = (lax.broadcasted_iota(jnp.int32, (CHUNK, CHUNK), 1)
              <= lax.broadcasted_iota(jnp.int32, (CHUNK, CHUNK), 0))
    hd = w // SGU_HEADS
    gcols = HEADS_PER_DOT * hd
    lane_head = lax.broadcasted_iota(jnp.int32, (CHUNK, gcols), 1) // hd
    bias = sgub_ref[layer]
    wgs = [jnp.concatenate([jnp.where(causal, sguw_ref[g * HEADS_PER_DOT + hh], 0.0).astype(BF16)
                            for hh in range(HEADS_PER_DOT)], axis=1)
           for g in range(SGU_HEADS // HEADS_PER_DOT)]
    mixed_rows = []
    for c in range(tm // CHUNK):
        vc = vb[c * CHUNK:(c + 1) * CHUNK, :]
        outs = []
        for g, wg in enumerate(wgs):
            vg = vc[:, g * gcols:(g + 1) * gcols]
            rhs = jnp.concatenate(
                [jnp.where(lane_head == hh, vg, jnp.zeros((), BF16)) for hh in range(HEADS_PER_DOT)], axis=0)
            outs.append(_dot(wg, rhs))
        mixed_rows.append(jnp.concatenate(outs, axis=1) + bias)
    mixed = jnp.concatenate(mixed_rows, axis=0)
    y_b = u * mixed * _silu(pb[:, 2 * w:3 * w])

    z = pa[:, w:2 * w] * pa[:, 2 * w:3 * w]
    zbuf[zoff:zoff + tm, :] = z
    zfull = zbuf[...]
    conv = p["conv_b"] + (_shift_rows(zfull, 2)[zoff:, :] * p["conv_w0"]
                          + _shift_rows(zfull, 1)[zoff:, :] * p["conv_w1"]
                          + z * p["conv_w2"])
    y_a = pa[:, 0:w] * conv * _silu(pa[:, 3 * w:4 * w])
    zbuf[0:zoff, :] = zbuf[tm:tm + zoff, :]

    convo_ref[0] = z[tm - (CONV_W - 1):tm, :]
    for j in range(POOL_BUF):
        r = coff + tm - POOL_BUF + j
        poolo_ref[j, pl.ds(b, 1), :] = cbuf[r:r + 1, :]
    vo_ref[0] = v[tm - CHUNK:tm, :]
    cbuf[0:coff, :] = cbuf[tm:tm + coff, :]

    da = _dot(y_a.astype(BF16), wbr_ref[0:w, :])
    db = _dot(y_b.astype(BF16), wbr_ref[w:2 * w, :])
    dc = _dot(y_c.astype(BF16), wbr_ref[2 * w:3 * w, :])
    merged = _sigmoid(pg[:, 0:d]) * da + _sigmoid(pg[:, d:2 * d]) * db + _sigmoid(pg[:, 2 * d:3 * d]) * dc
    x_new = x + gate * _dot(merged.astype(BF16), wout_ref[...])
    if final:
        ms = jnp.mean(x_new * x_new, axis=-1, keepdims=True)
        x_new = x_new * lax.rsqrt(ms + EPS) * fg_ref[...]
    xo_ref[0] = x_new


def _decode_rows(xs_ref, mods_ref, sconv_ref, spool_hbm, prm, sguw_ref, win_ref, poolw_ref, wbr_ref, wout_ref, fg_ref,
                 xso_ref, convso_ref, vso_ref, poolso_hbm, state, cx_stage, sem, *, layer, final):
    d = xs_ref.shape[-1]
    w = d // 2
    p = _layer_rows(prm, layer)
    gd = w // len(POOL_WINDOWS)

    load = pltpu.make_async_copy(spool_hbm.at[layer], state, sem.at[0])
    load.start()

    x = xs_ref[...] if len(xs_ref.shape) == 2 else xs_ref[:, 0, :]
    mod = mods_ref[...]
    shift, scale, gate = mod[:, 0:d], mod[:, d:2 * d], mod[:, 2 * d:3 * d]
    hb = _modulated_norm(x, p["norm_g"], shift, scale).astype(BF16)

    pa = _dot(hb, win_ref[:, 0:4 * w])
    z = pa[:, w:2 * w] * pa[:, 2 * w:3 * w]
    prev0, prev1 = sconv_ref[:, 0, :], sconv_ref[:, 1, :]
    conv = p["conv_b"] + (prev0 * p["conv_w0"] + prev1 * p["conv_w1"] + z * p["conv_w2"])
    y_a = pa[:, 0:w] * conv * _silu(pa[:, 3 * w:4 * w])
    convso_ref[:, 0, :] = prev1
    convso_ref[:, 1, :] = z

    pb = _dot(hb, win_ref[:, 4 * w:7 * w])
    u = _gelu(pb[:, 0:w])
    v = _layernorm(_gelu(pb[:, w:2 * w]), p["lnv_g"], p["lnv_b"])
    vso_ref[:, 0, :] = v
    sgu_w0 = _spread_heads([sguw_ref[h, 0:1, 0:1] for h in range(SGU_HEADS)], w)
    sgu_b0 = _spread_heads([prm[-1][layer, h:h + 1, 0:1] for h in range(SGU_HEADS)], w)
    y_b = u * (v * sgu_w0 + sgu_b0) * _silu(pb[:, 2 * w:3 * w])

    pc = _dot(hb, win_ref[:, 7 * w:9 * w])
    c_x = pc[:, 0:w]
    cx_stage[...] = c_x
    newest = pltpu.make_async_copy(cx_stage, poolso_hbm.at[layer, POOL_BUF - 1], sem.at[1])
    newest.start()
    load.wait()
    keep = pltpu.make_async_copy(state.at[pl.ds(1, POOL_BUF - 1)], poolso_hbm.at[layer, pl.ds(0, POOL_BUF - 1)],
                                 sem.at[2])
    keep.start()
    yc_groups = []
    for g, win in enumerate(POOL_WINDOWS):
        s = c_x[:, g * gd:(g + 1) * gd]
        for j in range(POOL_BUF - (win - 1), POOL_BUF):
            s = s + state[j, :, g * gd:(g + 1) * gd]
        pm = s / float(min(win, PAST_LEN + 1)) - c_x[:, g * gd:(g + 1) * gd]
        yc_groups.append(_dot(pm.astype(BF16), poolw_ref[g].astype(BF16)))
    y_c = jnp.concatenate(yc_groups, axis=1)
    y_c = (y_c + p["pool_b"]) * p["pool_scale"] * _silu(pc[:, w:2 * w])

    pg = _dot(hb, win_ref[:, 9 * w:9 * w + 3 * d])
    merged = (_sigmoid(pg[:, 0:d]) * _dot(y_a.astype(BF16), wbr_ref[0:w, :])
              + _sigmoid(pg[:, d:2 * d]) * _dot(y_b.astype(BF16), wbr_ref[w:2 * w, :])
              + _sigmoid(pg[:, 2 * d:3 * d]) * _dot(y_c.astype(BF16), wbr_ref[2 * w:3 * w, :]))
    x_new = x + gate * _dot(merged.astype(BF16), wout_ref[...])
    if final:
        ms = jnp.mean(x_new * x_new, axis=-1, keepdims=True)
        x_new = x_new * lax.rsqrt(ms + EPS) * fg_ref[...]
    if len(xso_ref.shape) == 2:
        xso_ref[...] = x_new
    else:
        xso_ref[:, 0, :] = x_new
    newest.wait()
    keep.wait()


def _layer_kernel(x_ref, mod_ref, win_ref, sguw_ref, sgub_ref, poolw_ref, wbr_ref, wout_ref, fg_ref,
                  xs_ref, mods_ref, sconv_ref, spool_hbm,
                  *rest, layer, final, n_tiles, prompt_steps):
    prm, rest = rest[:N_ROW_PARAMS], rest[N_ROW_PARAMS:]
    rest = rest[N_STATE_OUTPUTS:]
    if final:
        outs, scratch = rest[:8], rest[8:]
    else:
        casts_in, outs, casts_out, scratch = rest[:3], rest[3:11], rest[11:14], rest[14:]
    xo_ref, convo_ref, poolo_ref, vo_ref, xso_ref, convso_ref, vso_ref, poolso_hbm = outs
    zbuf, cbuf, state, cx_stage, sem = scratch
    step = pl.program_id(0)

    @pl.when(step < prompt_steps)
    def _():
        if not final:
            for src, dst in zip(casts_in, casts_out):
                dst[...] = src[...].astype(BF16)
        _prompt_tile(step // n_tiles, step % n_tiles, x_ref, mod_ref, prm, win_ref, sguw_ref, sgub_ref,
                     poolw_ref, wbr_ref, wout_ref, fg_ref, xo_ref, convo_ref, poolo_ref, vo_ref, zbuf, cbuf,
                     layer=layer, final=final)

    @pl.when(step == prompt_steps)
    def _():
        _decode_rows(xs_ref, mods_ref, sconv_ref, spool_hbm, prm, sguw_ref, win_ref, poolw_ref, wbr_ref, wout_ref,
                     fg_ref,
                     xso_ref, convso_ref, vso_ref, poolso_hbm, state, cx_stage, sem, layer=layer, final=final)


def _layer_call(layer, final, x, xs, mod_p, mod_s, sconv, spool, rows_p, sgub, fg, weights, sguw, poolw, next_f32,
                states):
    n, seq, d = x.shape
    rows = xs.shape[0]
    w = d // 2
    tm = SEQ_TILE
    n_tiles = seq // tm
    prompt_steps = n * n_tiles
    xs_out = jax.ShapeDtypeStruct((rows, 1, d) if final else (rows, d), F32)
    resident = dict(pipeline_mode=pl.Buffered(1))
    whole = lambda a: pl.BlockSpec(a.shape, lambda s: (0,) * a.ndim, **resident)
    of_layer = lambda a: pl.BlockSpec((None,) + a.shape[1:], lambda s: (layer,) + (0,) * (a.ndim - 1), **resident)
    win, wbr, wout = weights

    ptile = lambda s: jnp.minimum(s, prompt_steps - 1)
    x_tile = lambda s: (ptile(s) // n_tiles, ptile(s) % n_tiles, 0)
    batch_row = lambda s: (layer, ptile(s) // n_tiles, 0, 0)
    hbm = pl.BlockSpec(memory_space=pl.ANY)

    in_specs = [
        pl.BlockSpec((1, tm, d), x_tile),
        whole(mod_p), whole(win), of_layer(sguw), whole(sgub), of_layer(poolw), whole(wbr), whole(wout), whole(fg),
        whole(xs), of_layer(mod_s), of_layer(sconv), hbm,
        *[whole(a) for a in rows_p],
    ]
    out_specs = [
        pl.BlockSpec((1, tm, d), x_tile),
        pl.BlockSpec((None, 1, CONV_W - 1, w), batch_row),
        pl.BlockSpec((None, POOL_BUF, n, w), lambda s: (layer, 0, 0, 0)),
        pl.BlockSpec((None, 1, CHUNK, w), batch_row),
        pl.BlockSpec(xs_out.shape, lambda s: (0,) * len(xs_out.shape)),
        pl.BlockSpec((None, rows, CONV_W - 1, w), lambda s: (layer, 0, 0, 0)),
        pl.BlockSpec((None, rows, 1, w), lambda s: (layer, 0, 0, 0)),
        hbm,
    ]
    conv_p, pool_p, v_p, conv_s, v_s, pool_s = (jax.ShapeDtypeStruct(a.shape, F32) for a in states)
    out_shape = [jax.ShapeDtypeStruct((n, seq, d), F32), conv_p, pool_p, v_p,
                 xs_out, conv_s, v_s, pool_s]
    state_outputs = (1, 2, 3, 5, 6, 7)
    assert len(state_outputs) == len(states) == N_STATE_OUTPUTS
    assert len(rows_p) == N_ROW_PARAMS
    args = [x, mod_p, win, sguw, sgub, poolw, wbr, wout, fg, xs, mod_s, sconv, spool, *rows_p]
    aliases = {len(args) + i: o for i, o in enumerate(state_outputs)}
    in_specs += [hbm] * len(states)
    args += list(states)
    if not final:
        for src in next_f32:
            nrows, cols = src.shape[1:]
            assert nrows % prompt_steps == 0 and (nrows // prompt_steps) % BF16_SUBLANES == 0
            blk = nrows // prompt_steps
            in_specs.append(pl.BlockSpec((None, blk, cols), lambda s: (layer + 1, ptile(s), 0)))
            out_specs.append(pl.BlockSpec((blk, cols), lambda s: (ptile(s), 0)))
            out_shape.append(jax.ShapeDtypeStruct((nrows, cols), BF16))
            args.append(src)
    scratch_shapes = [
        pltpu.VMEM((SUBLANES + tm, w), F32),
        pltpu.VMEM((2 * SUBLANES + tm, w), F32),
        pltpu.VMEM((POOL_BUF, rows, w), F32),
        pltpu.VMEM((rows, w), F32),
        pltpu.SemaphoreType.DMA((3,)),
    ]
    return pl.pallas_call(
        functools.partial(_layer_kernel, layer=layer, final=final, n_tiles=n_tiles, prompt_steps=prompt_steps),
        grid=(prompt_steps + 1,),
        in_specs=in_specs,
        out_specs=out_specs,
        out_shape=out_shape,
        scratch_shapes=scratch_shapes,
        input_output_aliases=aliases,
        compiler_params=pltpu.CompilerParams(
            dimension_semantics=("arbitrary",),
            vmem_limit_bytes=VMEM_LIMIT_BYTES),
        name=f"layer{layer}",
    )(*args)


def kernel(x_prompt, x_sample, c_prompt, c_sample, state_conv, state_pool, w_ada, b_ada, norm_g, w_in, conv_w,
           conv_b, lnv_g, lnv_b, sgu_w, sgu_b, pool_w, pool_b, pool_scale, w_branch, w_out, final_g):
    n, seq, d = x_prompt.shape
    rows = x_sample.shape[0]
    depth = w_in.shape[0]
    w = d // 2
    hd = w // SGU_HEADS
    assert x_sample.shape[1] == 1 and seq % SEQ_TILE == 0 and SEQ_TILE % CHUNK == 0

    next_f32 = (w_in, w_branch.reshape(depth, N_BRANCH * w, d), w_out)
    state_shapes = ((depth, n, CONV_W - 1, w), (depth, POOL_BUF, n, w), (depth, n, CHUNK, w),
                    (depth, rows, CONV_W - 1, w), (depth, rows, 1, w), (depth, POOL_BUF, rows, w))
    mod_p, mod_s, *made = _mod_call(c_prompt, c_sample, w_ada, b_ada, next_f32, state_shapes)
    weights, states = made[:len(next_f32)], tuple(made[len(next_f32):])

    rows_p = (norm_g, conv_b, lnv_g, lnv_b, pool_b, pool_scale, jnp.transpose(conv_w, (1, 0, 2)), sgu_b)
    sgub_full = jnp.repeat(jnp.swapaxes(sgu_b, 1, 2), hd, axis=2)
    fg = final_g.reshape(1, d)
    hist_major = lambda a: jnp.transpose(a, (0, 2, 1, 3))
    spool = hist_major(state_pool)

    xp, xs = x_prompt, x_sample
    for l in range(depth):
        xp, conv_p, pool_p, v_p, xs, conv_s, v_s, pool_s, *next_weights = _layer_call(
            l, l == depth - 1, xp, xs, mod_p, mod_s, state_conv, spool, rows_p, sgub_full, fg, tuple(weights), sgu_w,
            pool_w, next_f32, states)
        states = (conv_p, pool_p, v_p, conv_s, v_s, pool_s)
        weights = next_weights

    return xp, xs, conv_p, conv_s, hist_major(pool_p), hist_major(pool_s), v_p, v_s
```

```python
import functools
import math

import jax
import jax.numpy as jnp
from jax import lax
from jax.experimental import pallas as pl
from jax.experimental.pallas import tpu as pltpu

F32 = jnp.float32
BF16 = jnp.bfloat16

CONV_W = 3
CHUNK = 128
SGU_HEADS = 8
POOL_WINDOWS = (2, 4, 8, 16)
POOL_BUF = max(POOL_WINDOWS) - 1
PAST_LEN = 16384
EPS = 1e-6

N_BRANCH = 3
N_ROW_PARAMS = 8
N_STATE_OUTPUTS = 6
SUBLANES = 8
BF16_SUBLANES = 16
SEQ_TILE = 512
HEADS_PER_DOT = 2
VMEM_LIMIT_BYTES = 58 * 1024 * 1024


def _sigmoid(x):
    return 0.5 * (1.0 + jnp.tanh(0.5 * x))


def _silu(x):
    return x * _sigmoid(x)


def _gelu(x):
    c = math.sqrt(2.0 / math.pi)
    return 0.5 * x * (1.0 + jnp.tanh(c * (x + 0.044715 * (x * x * x))))


def _shift_rows(x, k):
    return pltpu.roll(x, k, axis=0)


def _dot(a, b):
    return jnp.dot(a, b, preferred_element_type=F32)


def _modulated_norm(x, norm_g, shift, scale):
    ms = jnp.mean(x * x, axis=-1, keepdims=True)
    return (x * lax.rsqrt(ms + EPS)) * (norm_g * (1.0 + scale)) + shift


def _layer_rows(prm, layer):
    ng, convb, lng, lnb, poolb, pools, convw, _ = prm
    row = lambda ref: ref[layer:layer + 1, :]
    out = dict(norm_g=row(ng), conv_b=row(convb), lnv_g=row(lng), lnv_b=row(lnb), pool_b=row(poolb),
               pool_scale=row(pools))
    for k in range(CONV_W):
        out[f"conv_w{k}"] = convw[k, layer:layer + 1, :]
    return out


def _spread_heads(per_head, w):
    hd = w // len(per_head)
    rows = per_head[0].shape[0]
    lane_head = lax.broadcasted_iota(jnp.int32, (rows, w), 1) // hd
    out = jnp.zeros((rows, w), F32)
    for h, val in enumerate(per_head):
        out = jnp.where(lane_head == h, val, out)
    return out


def _layernorm(x, g, b):
    mu = jnp.mean(x, axis=-1, keepdims=True)
    d = x - mu
    var = jnp.mean(d * d, axis=-1, keepdims=True)
    return d * lax.rsqrt(var + EPS) * g + b


def _mod_kernel(cp_ref, cs_ref, w_ref, b_ref, sgub_ref, *rest, n_cast):
    casts_in, (op_ref, os_ref, bias_ref), rest = rest[:n_cast], rest[n_cast:n_cast + 3], rest[n_cast + 3:]
    casts_out, state_refs = rest[:n_cast], rest[n_cast:]
    by_position = sgub_ref[pl.program_id(0)].T
    bias_ref[0] = _spread_heads([by_position[:, h:h + 1] for h in range(SGU_HEADS)], bias_ref.shape[-1])
    for ref in state_refs:
        ref[...] = jnp.zeros(ref.shape, ref.dtype)
    n = cp_ref.shape[0]
    c = jnp.concatenate([cp_ref[...], cp_ref[...], cs_ref[...]], axis=0)
    m = _dot(_silu(c).astype(BF16), w_ref[0].astype(BF16)) + b_ref[pl.ds(pl.program_id(0), 1), :]
    op_ref[0] = m[0:n]
    os_ref[0] = m[2 * n:]
    for src, dst in zip(casts_in, casts_out):
        dst[...] = src[...].astype(BF16)


def _mod_call(c_prompt, c_sample, w_ada, b_ada, sgu_b, first_f32, state_shapes):
    depth, d, d3 = w_ada.shape
    chunk = sgu_b.shape[2]
    n, rows = c_prompt.shape[0], c_sample.shape[0]
    assert n == SUBLANES
    n_col = 8
    col_block = d3 // n_col
    steps = depth * n_col
    in_specs = [
        pl.BlockSpec((n, d), lambda l, j: (0, 0)),
        pl.BlockSpec((rows, d), lambda l, j: (0, 0)),
        pl.BlockSpec((1, d, col_block), lambda l, j: (l, 0, j)),
        pl.BlockSpec((depth, col_block), lambda l, j: (0, j)),
        pl.BlockSpec(sgu_b.shape, lambda l, j: (0, 0, 0)),
    ]
    out_specs = [
        pl.BlockSpec((1, n, col_block), lambda l, j: (l, 0, j)),
        pl.BlockSpec((1, rows, col_block), lambda l, j: (l, 0, j)),
        pl.BlockSpec((1, chunk, d // 2), lambda l, j: (l, 0, 0)),
    ]
    out_shape = [
        jax.ShapeDtypeStruct((depth, n, d3), F32),
        jax.ShapeDtypeStruct((depth, rows, d3), F32),
        jax.ShapeDtypeStruct((depth, chunk, d // 2), F32),
    ]
    for src in first_f32:
        nrows, cols = src.shape[1:]
        assert nrows % steps == 0 and (nrows // steps) % BF16_SUBLANES == 0
        blk = nrows // steps
        in_specs.append(pl.BlockSpec((None, blk, cols), lambda l, j: (0, l * n_col + j, 0)))
        out_specs.append(pl.BlockSpec((blk, cols), lambda l, j: (l * n_col + j, 0)))
        out_shape.append(jax.ShapeDtypeStruct((nrows, cols), BF16))
    for shape in state_shapes:
        assert shape[0] == depth and len(shape) == 4
        if shape[1] % n_col == 0:
            axis = 1
        elif (shape[2] // n_col) % SUBLANES == 0 and shape[2] % n_col == 0:
            axis = 2
        else:
            axis = None
        block = tuple(1 if a == 0 else (s // n_col if a == axis else s) for a, s in enumerate(shape))
        out_specs.append(pl.BlockSpec(block, lambda l, j, axis=axis: tuple(
            l if a == 0 else (j if a == axis else 0) for a in range(4))))
        out_shape.append(jax.ShapeDtypeStruct(shape, F32))
    return pl.pallas_call(
        functools.partial(_mod_kernel, n_cast=len(first_f32)),
        grid=(depth, n_col),
        in_specs=in_specs,
        out_specs=out_specs,
        out_shape=out_shape,
        compiler_params=pltpu.CompilerParams(
            dimension_semantics=("arbitrary", "arbitrary"),
            vmem_limit_bytes=VMEM_LIMIT_BYTES),
        name="adaln_mod",
    )(c_prompt, c_sample, w_ada, b_ada, sgu_b, *first_f32)


def _prompt_tile(b, t, x_ref, mod_ref, prm, win_ref, sguw_ref, sgub_ref, poolw_ref, wbr_ref, wout_ref, fg_ref,
                 xo_ref, convo_ref, poolo_ref, vo_ref, zbuf, cbuf, *, layer, final):
    tm, d = x_ref.shape[1], x_ref.shape[2]
    w = d // 2
    p = _layer_rows(prm, layer)
    zoff = SUBLANES
    coff = 2 * SUBLANES

    @pl.when(t == 0)
    def _():
        zbuf[0:zoff, :] = jnp.zeros((zoff, w), F32)
        cbuf[0:coff, :] = jnp.zeros((coff, w), F32)

    x = x_ref[0]
    mod = mod_ref[layer, pl.ds(b, 1), :]
    shift, scale, gate = mod[:, 0:d], mod[:, d:2 * d], mod[:, 2 * d:3 * d]
    hb = _modulated_norm(x, p["norm_g"], shift, scale).astype(BF16)

    pc = _dot(hb, win_ref[:, 7 * w:9 * w])
    pb = _dot(hb, win_ref[:, 4 * w:7 * w])
    pa = _dot(hb, win_ref[:, 0:4 * w])
    pg = _dot(hb, win_ref[:, 9 * w:9 * w + 3 * d])

    c_x = pc[:, 0:w]
    cbuf[coff:coff + tm, :] = c_x
    gd = w // len(POOL_WINDOWS)
    pos1 = lax.broadcasted_iota(jnp.int32, (tm, gd), 0) + (t * tm + 1)
    yc_groups = []
    for g, win in enumerate(POOL_WINDOWS):
        cols = slice(g * gd, (g + 1) * gd)
        s = cbuf[:, cols]
        k = 1
        while k < win:
            s = s + _shift_rows(s, k)
            k *= 2
        s = s[coff:, :]
        cnt = jnp.minimum(pos1, win).astype(F32)
        pm = s / cnt - c_x[:, cols]
        yc_groups.append(_dot(pm.astype(BF16), poolw_ref[g].astype(BF16)))
    y_c = jnp.concatenate(yc_groups, axis=1)
    y_c = (y_c + p["pool_b"]) * p["pool_scale"] * _silu(pc[:, w:2 * w])

    u = _gelu(pb[:, 0:w])
    v = _layernorm(_gelu(pb[:, w:2 * w]), p["lnv_g"], p["lnv_b"])
    vb = v.astype(BF16)
    causal = (lax.broadcasted_iota(jnp.int32, (CHUNK, CHUNK), 1)
              <= lax.broadcasted_iota(jnp.int32, (CHUNK, CHUNK), 0))
    hd = w // SGU_HEADS
    gcols = HEADS_PER_DOT * hd
    lane_head = lax.broadcasted_iota(jnp.int32, (CHUNK, gcols), 1) // hd
    bias = sgub_ref[layer]
    wgs = [jnp.concatenate([jnp.where(causal, sguw_ref[g * HEADS_PER_DOT + hh], 0.0).astype(BF16)
                            for hh in range(HEADS_PER_DOT)], axis=1)
           for g in range(SGU_HEADS // HEADS_PER_DOT)]
    mixed_rows = []
    for c in range(tm // CHUNK):
        vc = vb[c * CHUNK:(c + 1) * CHUNK, :]
        outs = []
        for g, wg in enumerate(wgs):
            vg = vc[:, g * gcols:(g + 1) * gcols]
            rhs = jnp.concatenate(
                [jnp.where(lane_head == hh, vg, jnp.zeros((), BF16)) for hh in range(HEADS_PER_DOT)], axis=0)
            outs.append(_dot(wg, rhs))
        mixed_rows.append(jnp.concatenate(outs, axis=1) + bias)
    mixed = jnp.concatenate(mixed_rows, axis=0)
    y_b = u * mixed * _silu(pb[:, 2 * w:3 * w])

    z = pa[:, w:2 * w] * pa[:, 2 * w:3 * w]
    zbuf[zoff:zoff + tm, :] = z
    zfull = zbuf[...]
    conv = p["conv_b"] + (_shift_rows(zfull, 2)[zoff:, :] * p["conv_w0"]
                          + _shift_rows(zfull, 1)[zoff:, :] * p["conv_w1"]
                          + z * p["conv_w2"])
    y_a = pa[:, 0:w] * conv * _silu(pa[:, 3 * w:4 * w])
    zbuf[0:zoff, :] = zbuf[tm:tm + zoff, :]

    convo_ref[0] = z[tm - (CONV_W - 1):tm, :]
    for j in range(POOL_BUF):
        r = coff + tm - POOL_BUF + j
        poolo_ref[j, pl.ds(b, 1), :] = cbuf[r:r + 1, :]
    vo_ref[0] = v[tm - CHUNK:tm, :]
    cbuf[0:coff, :] = cbuf[tm:tm + coff, :]

    da = _dot(y_a.astype(BF16), wbr_ref[0:w, :])
    db = _dot(y_b.astype(BF16), wbr_ref[w:2 * w, :])
    dc = _dot(y_c.astype(BF16), wbr_ref[2 * w:3 * w, :])
    merged = _sigmoid(pg[:, 0:d]) * da + _sigmoid(pg[:, d:2 * d]) * db + _sigmoid(pg[:, 2 * d:3 * d]) * dc
    x_new = x + gate * _dot(merged.astype(BF16), wout_ref[...])
    if final:
        ms = jnp.mean(x_new * x_new, axis=-1, keepdims=True)
        x_new = x_new * lax.rsqrt(ms + EPS) * fg_ref[...]
    xo_ref[0] = x_new


def _decode_rows(xs_ref, mods_ref, sconv_ref, spool_hbm, prm, sguw_ref, win_ref, poolw_ref, wbr_ref, wout_ref, fg_ref,
                 xso_ref, convso_ref, vso_ref, poolso_hbm, state, cx_stage, sem, *, layer, final):
    d = xs_ref.shape[-1]
    w = d // 2
    p = _layer_rows(prm, layer)
    gd = w // len(POOL_WINDOWS)

    load = pltpu.make_async_copy(spool_hbm.at[layer], state, sem.at[0])
    load.start()

    x = xs_ref[...] if len(xs_ref.shape) == 2 else xs_ref[:, 0, :]
    mod = mods_ref[...]
    shift, scale, gate = mod[:, 0:d], mod[:, d:2 * d], mod[:, 2 * d:3 * d]
    hb = _modulated_norm(x, p["norm_g"], shift, scale).astype(BF16)

    pa = _dot(hb, win_ref[:, 0:4 * w])
    z = pa[:, w:2 * w] * pa[:, 2 * w:3 * w]
    prev0, prev1 = sconv_ref[:, 0, :], sconv_ref[:, 1, :]
    conv = p["conv_b"] + (prev0 * p["conv_w0"] + prev1 * p["conv_w1"] + z * p["conv_w2"])
    y_a = pa[:, 0:w] * conv * _silu(pa[:, 3 * w:4 * w])
    convso_ref[:, 0, :] = prev1
    convso_ref[:, 1, :] = z

    pb = _dot(hb, win_ref[:, 4 * w:7 * w])
    u = _gelu(pb[:, 0:w])
    v = _layernorm(_gelu(pb[:, w:2 * w]), p["lnv_g"], p["lnv_b"])
    vso_ref[:, 0, :] = v
    sgu_w0 = _spread_heads([sguw_ref[h, 0:1, 0:1] for h in range(SGU_HEADS)], w)
    sgu_b0 = _spread_heads([prm[-1][layer, h:h + 1, 0:1] for h in range(SGU_HEADS)], w)
    y_b = u * (v * sgu_w0 + sgu_b0) * _silu(pb[:, 2 * w:3 * w])

    pc = _dot(hb, win_ref[:, 7 * w:9 * w])
    c_x = pc[:, 0:w]
    cx_stage[...] = c_x
    newest = pltpu.make_async_copy(cx_stage, poolso_hbm.at[layer, POOL_BUF - 1], sem.at[1])
    newest.start()
    load.wait()
    keep = pltpu.make_async_copy(state.at[pl.ds(1, POOL_BUF - 1)], poolso_hbm.at[layer, pl.ds(0, POOL_BUF - 1)],
                                 sem.at[2])
    keep.start()
    yc_groups = []
    for g, win in enumerate(POOL_WINDOWS):
        s = c_x[:, g * gd:(g + 1) * gd]
        for j in range(POOL_BUF - (win - 1), POOL_BUF):
            s = s + state[j, :, g * gd:(g + 1) * gd]
        pm = s / float(min(win, PAST_LEN + 1)) - c_x[:, g * gd:(g + 1) * gd]
        yc_groups.append(_dot(pm.astype(BF16), poolw_ref[g].astype(BF16)))
    y_c = jnp.concatenate(yc_groups, axis=1)
    y_c = (y_c + p["pool_b"]) * p["pool_scale"] * _silu(pc[:, w:2 * w])

    pg = _dot(hb, win_ref[:, 9 * w:9 * w + 3 * d])
    merged = (_sigmoid(pg[:, 0:d]) * _dot(y_a.astype(BF16), wbr_ref[0:w, :])
              + _sigmoid(pg[:, d:2 * d]) * _dot(y_b.astype(BF16), wbr_ref[w:2 * w, :])
              + _sigmoid(pg[:, 2 * d:3 * d]) * _dot(y_c.astype(BF16), wbr_ref[2 * w:3 * w, :]))
    x_new = x + gate * _dot(merged.astype(BF16), wout_ref[...])
    if final:
        ms = jnp.mean(x_new * x_new, axis=-1, keepdims=True)
        x_new = x_new * lax.rsqrt(ms + EPS) * fg_ref[...]
    if len(xso_ref.shape) == 2:
        xso_ref[...] = x_new
    else:
        xso_ref[:, 0, :] = x_new
    newest.wait()
    keep.wait()


def _layer_kernel(x_ref, mod_ref, win_ref, sguw_ref, sgub_ref, poolw_ref, wbr_ref, wout_ref, fg_ref,
                  xs_ref, mods_ref, sconv_ref, spool_hbm,
                  *rest, layer, final, n_tiles, prompt_steps):
    prm, rest = rest[:N_ROW_PARAMS], rest[N_ROW_PARAMS:]
    rest = rest[N_STATE_OUTPUTS:]
    if final:
        outs, scratch = rest[:8], rest[8:]
    else:
        casts_in, outs, casts_out, scratch = rest[:3], rest[3:11], rest[11:14], rest[14:]
    xo_ref, convo_ref, poolo_ref, vo_ref, xso_ref, convso_ref, vso_ref, poolso_hbm = outs
    zbuf, cbuf, state, cx_stage, sem = scratch
    step = pl.program_id(0)

    @pl.when(step < prompt_steps)
    def _():
        if not final:
            for src, dst in zip(casts_in, casts_out):
                dst[...] = src[...].astype(BF16)
        _prompt_tile(step // n_tiles, step % n_tiles, x_ref, mod_ref, prm, win_ref, sguw_ref, sgub_ref,
                     poolw_ref, wbr_ref, wout_ref, fg_ref, xo_ref, convo_ref, poolo_ref, vo_ref, zbuf, cbuf,
                     layer=layer, final=final)

    @pl.when(step == prompt_steps)
    def _():
        _decode_rows(xs_ref, mods_ref, sconv_ref, spool_hbm, prm, sguw_ref, win_ref, poolw_ref, wbr_ref, wout_ref,
                     fg_ref,
                     xso_ref, convso_ref, vso_ref, poolso_hbm, state, cx_stage, sem, layer=layer, final=final)


def _layer_call(layer, final, x, xs, mod_p, mod_s, sconv, spool, rows_p, sgub, fg, weights, sguw, poolw, next_f32,
                states):
    n, seq, d = x.shape
    rows = xs.shape[0]
    w = d // 2
    tm = SEQ_TILE
    n_tiles = seq // tm
    prompt_steps = n * n_tiles
    xs_out = jax.ShapeDtypeStruct((rows, 1, d) if final else (rows, d), F32)
    resident = dict(pipeline_mode=pl.Buffered(1))
    whole = lambda a: pl.BlockSpec(a.shape, lambda s: (0,) * a.ndim, **resident)
    of_layer = lambda a: pl.BlockSpec((None,) + a.shape[1:], lambda s: (layer,) + (0,) * (a.ndim - 1), **resident)
    win, wbr, wout = weights

    ptile = lambda s: jnp.minimum(s, prompt_steps - 1)
    x_tile = lambda s: (ptile(s) // n_tiles, ptile(s) % n_tiles, 0)
    batch_row = lambda s: (layer, ptile(s) // n_tiles, 0, 0)
    hbm = pl.BlockSpec(memory_space=pl.ANY)

    in_specs = [
        pl.BlockSpec((1, tm, d), x_tile),
        whole(mod_p), whole(win), of_layer(sguw), whole(sgub), of_layer(poolw), whole(wbr), whole(wout), whole(fg),
        whole(xs), of_layer(mod_s), of_layer(sconv), hbm,
        *[whole(a) for a in rows_p],
    ]
    out_specs = [
        pl.BlockSpec((1, tm, d), x_tile),
        pl.BlockSpec((None, 1, CONV_W - 1, w), batch_row),
        pl.BlockSpec((None, POOL_BUF, n, w), lambda s: (layer, 0, 0, 0)),
        pl.BlockSpec((None, 1, CHUNK, w), batch_row),
        pl.BlockSpec(xs_out.shape, lambda s: (0,) * len(xs_out.shape)),
        pl.BlockSpec((None, rows, CONV_W - 1, w), lambda s: (layer, 0, 0, 0)),
        pl.BlockSpec((None, rows, 1, w), lambda s: (layer, 0, 0, 0)),
        hbm,
    ]
    conv_p, pool_p, v_p, conv_s, v_s, pool_s = (jax.ShapeDtypeStruct(a.shape, F32) for a in states)
    out_shape = [jax.ShapeDtypeStruct((n, seq, d), F32), conv_p, pool_p, v_p,
                 xs_out, conv_s, v_s, pool_s]
    state_outputs = (1, 2, 3, 5, 6, 7)
    assert len(state_outputs) == len(states) == N_STATE_OUTPUTS
    assert len(rows_p) == N_ROW_PARAMS
    args = [x, mod_p, win, sguw, sgub, poolw, wbr, wout, fg, xs, mod_s, sconv, spool, *rows_p]
    aliases = {len(args) + i: o for i, o in enumerate(state_outputs)}
    in_specs += [hbm] * len(states)
    args += list(states)
    if not final:
        for src in next_f32:
            nrows, cols = src.shape[1:]
            assert nrows % prompt_steps == 0 and (nrows // prompt_steps) % BF16_SUBLANES == 0
            blk = nrows // prompt_steps
            in_specs.append(pl.BlockSpec((None, blk, cols), lambda s: (layer + 1, ptile(s), 0)))
            out_specs.append(pl.BlockSpec((blk, cols), lambda s: (ptile(s), 0)))
            out_shape.append(jax.ShapeDtypeStruct((nrows, cols), BF16))
            args.append(src)
    scratch_shapes = [
        pltpu.VMEM((SUBLANES + tm, w), F32),
        pltpu.VMEM((2 * SUBLANES + tm, w), F32),
        pltpu.VMEM((POOL_BUF, rows, w), F32),
        pltpu.VMEM((rows, w), F32),
        pltpu.SemaphoreType.DMA((3,)),
    ]
    return pl.pallas_call(
        functools.partial(_layer_kernel, layer=layer, final=final, n_tiles=n_tiles, prompt_steps=prompt_steps),
        grid=(prompt_steps + 1,),
        in_specs=in_specs,
        out_specs=out_specs,
        out_shape=out_shape,
        scratch_shapes=scratch_shapes,
        input_output_aliases=aliases,
        compiler_params=pltpu.CompilerParams(
            dimension_semantics=("arbitrary",),
            vmem_limit_bytes=VMEM_LIMIT_BYTES),
        name=f"layer{layer}",
    )(*args)


def kernel(x_prompt, x_sample, c_prompt, c_sample, state_conv, state_pool, w_ada, b_ada, norm_g, w_in, conv_w,
           conv_b, lnv_g, lnv_b, sgu_w, sgu_b, pool_w, pool_b, pool_scale, w_branch, w_out, final_g):
    n, seq, d = x_prompt.shape
    rows = x_sample.shape[0]
    depth = w_in.shape[0]
    w = d // 2
    assert x_sample.shape[1] == 1 and seq % SEQ_TILE == 0 and SEQ_TILE % CHUNK == 0

    next_f32 = (w_in, w_branch.reshape(depth, N_BRANCH * w, d), w_out)
    state_shapes = ((depth, n, CONV_W - 1, w), (depth, POOL_BUF, n, w), (depth, n, CHUNK, w),
                    (depth, rows, CONV_W - 1, w), (depth, rows, 1, w), (depth, POOL_BUF, rows, w))
    mod_p, mod_s, sgub_full, *made = _mod_call(c_prompt, c_sample, w_ada, b_ada, sgu_b, next_f32, state_shapes)
    weights, states = made[:len(next_f32)], tuple(made[len(next_f32):])

    rows_p = (norm_g, conv_b, lnv_g, lnv_b, pool_b, pool_scale, jnp.transpose(conv_w, (1, 0, 2)), sgu_b)
    fg = final_g.reshape(1, d)
    hist_major = lambda a: jnp.transpose(a, (0, 2, 1, 3))
    spool = hist_major(state_pool)

    xp, xs = x_prompt, x_sample
    for l in range(depth):
        xp, conv_p, pool_p, v_p, xs, conv_s, v_s, pool_s, *next_weights = _layer_call(
            l, l == depth - 1, xp, xs, mod_p, mod_s, state_conv, spool, rows_p, sgub_full, fg, tuple(weights), sgu_w,
            pool_w, next_f32, states)
        states = (conv_p, pool_p, v_p, conv_s, v_s, pool_s)
        weights = next_weights

    return xp, xs, conv_p, conv_s, hist_major(pool_p), hist_major(pool_s), v_p, v_s
```

```python
import functools
import math

import jax
import jax.numpy as jnp
from jax import lax
from jax.experimental import pallas as pl
from jax.experimental.pallas import tpu as pltpu

F32 = jnp.float32
BF16 = jnp.bfloat16

CONV_W = 3
CHUNK = 128
SGU_HEADS = 8
POOL_WINDOWS = (2, 4, 8, 16)
POOL_BUF = max(POOL_WINDOWS) - 1
PAST_LEN = 16384
EPS = 1e-6

N_BRANCH = 3
N_ROW_PARAMS = 8
N_STATE_OUTPUTS = 6
SUBLANES = 8
BF16_SUBLANES = 16
SEQ_TILE = 512
HEADS_PER_DOT = 2
VMEM_LIMIT_BYTES = 58 * 1024 * 1024


def _sigmoid(x):
    return 0.5 * (1.0 + jnp.tanh(0.5 * x))


def _silu(x):
    return x * _sigmoid(x)


def _gelu(x):
    c = math.sqrt(2.0 / math.pi)
    return 0.5 * x * (1.0 + jnp.tanh(c * (x + 0.044715 * (x * x * x))))


def _shift_rows(x, k):
    return pltpu.roll(x, k, axis=0)


def _dot(a, b):
    return jnp.dot(a, b, preferred_element_type=F32)


def _modulated_norm(x, norm_g, shift, scale):
    ms = jnp.mean(x * x, axis=-1, keepdims=True)
    return (x * lax.rsqrt(ms + EPS)) * (norm_g * (1.0 + scale)) + shift


def _layer_rows(prm, layer):
    ng, convb, lng, lnb, poolb, pools, convw, _ = prm
    row = lambda ref: ref[layer:layer + 1, :]
    out = dict(norm_g=row(ng), conv_b=row(convb), lnv_g=row(lng), lnv_b=row(lnb), pool_b=row(poolb),
               pool_scale=row(pools))
    for k in range(CONV_W):
        out[f"conv_w{k}"] = convw[k, layer:layer + 1, :]
    return out


def _spread_heads(per_head, w):
    hd = w // len(per_head)
    rows = per_head[0].shape[0]
    lane_head = lax.broadcasted_iota(jnp.int32, (rows, w), 1) // hd
    out = jnp.zeros((rows, w), F32)
    for h, val in enumerate(per_head):
        out = jnp.where(lane_head == h, val, out)
    return out


def _layernorm(x, g, b):
    mu = jnp.mean(x, axis=-1, keepdims=True)
    d = x - mu
    var = jnp.mean(d * d, axis=-1, keepdims=True)
    return d * lax.rsqrt(var + EPS) * g + b


def _mod_kernel(cp_ref, cs_ref, w_ref, b_ref, sgub_ref, *rest, n_cast):
    casts_in, (op_ref, os_ref, bias_ref), rest = rest[:n_cast], rest[n_cast:n_cast + 3], rest[n_cast + 3:]
    casts_out, state_refs = rest[:n_cast], rest[n_cast:]
    by_position = sgub_ref[pl.program_id(0)].T
    bias_ref[0] = _spread_heads([by_position[:, h:h + 1] for h in range(SGU_HEADS)], bias_ref.shape[-1])
    for ref in state_refs:
        ref[...] = jnp.zeros(ref.shape, ref.dtype)
    n = cp_ref.shape[0]
    c = jnp.concatenate([cp_ref[...], cp_ref[...], cs_ref[...]], axis=0)
    m = _dot(_silu(c).astype(BF16), w_ref[0].astype(BF16)) + b_ref[pl.ds(pl.program_id(0), 1), :]
    op_ref[0] = m[0:n]
    os_ref[0] = m[2 * n:]
    for src, dst in zip(casts_in, casts_out):
        dst[...] = src[...].astype(BF16)


def _mod_call(c_prompt, c_sample, w_ada, b_ada, sgu_b, first_f32, state_shapes):
    depth, d, d3 = w_ada.shape
    chunk = sgu_b.shape[2]
    n, rows = c_prompt.shape[0], c_sample.shape[0]
    assert n == SUBLANES
    n_col = 4
    col_block = d3 // n_col
    steps = depth * n_col
    in_specs = [
        pl.BlockSpec((n, d), lambda l, j: (0, 0)),
        pl.BlockSpec((rows, d), lambda l, j: (0, 0)),
        pl.BlockSpec((1, d, col_block), lambda l, j: (l, 0, j)),
        pl.BlockSpec((depth, col_block), lambda l, j: (0, j)),
        pl.BlockSpec(sgu_b.shape, lambda l, j: (0, 0, 0)),
    ]
    out_specs = [
        pl.BlockSpec((1, n, col_block), lambda l, j: (l, 0, j)),
        pl.BlockSpec((1, rows, col_block), lambda l, j: (l, 0, j)),
        pl.BlockSpec((1, chunk, d // 2), lambda l, j: (l, 0, 0)),
    ]
    out_shape = [
        jax.ShapeDtypeStruct((depth, n, d3), F32),
        jax.ShapeDtypeStruct((depth, rows, d3), F32),
        jax.ShapeDtypeStruct((depth, chunk, d // 2), F32),
    ]
    for src in first_f32:
        nrows, cols = src.shape[1:]
        assert nrows % steps == 0 and (nrows // steps) % BF16_SUBLANES == 0
        blk = nrows // steps
        in_specs.append(pl.BlockSpec((None, blk, cols), lambda l, j: (0, l * n_col + j, 0)))
        out_specs.append(pl.BlockSpec((blk, cols), lambda l, j: (l * n_col + j, 0)))
        out_shape.append(jax.ShapeDtypeStruct((nrows, cols), BF16))
    for shape in state_shapes:
        assert shape[0] == depth and len(shape) == 4
        if shape[1] % n_col == 0:
            axis = 1
        elif (shape[2] // n_col) % SUBLANES == 0 and shape[2] % n_col == 0:
            axis = 2
        else:
            axis = None
        block = tuple(1 if a == 0 else (s // n_col if a == axis else s) for a, s in enumerate(shape))
        out_specs.append(pl.BlockSpec(block, lambda l, j, axis=axis: tuple(
            l if a == 0 else (j if a == axis else 0) for a in range(4))))
        out_shape.append(jax.ShapeDtypeStruct(shape, F32))
    return pl.pallas_call(
        functools.partial(_mod_kernel, n_cast=len(first_f32)),
        grid=(depth, n_col),
        in_specs=in_specs,
        out_specs=out_specs,
        out_shape=out_shape,
        compiler_params=pltpu.CompilerParams(
            dimension_semantics=("arbitrary", "arbitrary"),
            vmem_limit_bytes=VMEM_LIMIT_BYTES),
        name="adaln_mod",
    )(c_prompt, c_sample, w_ada, b_ada, sgu_b, *first_f32)


def _prompt_tile(b, t, x_ref, mod_ref, prm, win_ref, sguw_ref, sgub_ref, poolw_ref, wbr_ref, wout_ref, fg_ref,
                 xo_ref, convo_ref, poolo_ref, vo_ref, zbuf, cbuf, shiftw, *, layer, final):
    tm, d = x_ref.shape[1], x_ref.shape[2]
    w = d // 2
    p = _layer_rows(prm, layer)
    zoff = SUBLANES
    coff = 2 * SUBLANES

    @pl.when(t == 0)
    def _():
        zbuf[0:zoff, :] = jnp.zeros((zoff, w), F32)
        cbuf[0:coff, :] = jnp.zeros((coff, w), F32)

    @pl.when((b == 0) & (t == 0))
    def _():
        shifts = mod_ref[layer, :, 0:d]
        both = jnp.concatenate([shifts, shifts], axis=0).astype(BF16)
        shiftw[...] = _dot(both, win_ref[...])[0:shifts.shape[0], :]

    x = x_ref[0]
    mod = mod_ref[layer, pl.ds(b, 1), :]
    scale, gate = mod[:, d:2 * d], mod[:, 2 * d:3 * d]
    hb = (x * (p["norm_g"] * (1.0 + scale))).astype(BF16)
    rs = lax.rsqrt(jnp.mean(x * x, axis=-1, keepdims=True) + EPS)

    def project(lo, hi):
        return rs * _dot(hb, win_ref[:, lo:hi]) + shiftw[pl.ds(b, 1), lo:hi]

    pc = project(7 * w, 9 * w)
    pb = project(4 * w, 7 * w)
    pa = project(0, 4 * w)
    pg = project(9 * w, 9 * w + 3 * d)

    c_x = pc[:, 0:w]
    cbuf[coff:coff + tm, :] = c_x
    gd = w // len(POOL_WINDOWS)
    pos1 = lax.broadcasted_iota(jnp.int32, (tm, gd), 0) + (t * tm + 1)
    yc_groups = []
    for g, win in enumerate(POOL_WINDOWS):
        cols = slice(g * gd, (g + 1) * gd)
        s = cbuf[:, cols]
        k = 1
        while k < win:
            s = s + _shift_rows(s, k)
            k *= 2
        s = s[coff:, :]
        cnt = jnp.minimum(pos1, win).astype(F32)
        pm = s / cnt - c_x[:, cols]
        yc_groups.append(_dot(pm.astype(BF16), poolw_ref[g].astype(BF16)))
    y_c = jnp.concatenate(yc_groups, axis=1)
    y_c = (y_c + p["pool_b"]) * p["pool_scale"] * _silu(pc[:, w:2 * w])

    u = _gelu(pb[:, 0:w])
    v = _layernorm(_gelu(pb[:, w:2 * w]), p["lnv_g"], p["lnv_b"])
    vb = v.astype(BF16)
    causal = (lax.broadcasted_iota(jnp.int32, (CHUNK, CHUNK), 1)
              <= lax.broadcasted_iota(jnp.int32, (CHUNK, CHUNK), 0))
    hd = w // SGU_HEADS
    gcols = HEADS_PER_DOT * hd
    lane_head = lax.broadcasted_iota(jnp.int32, (CHUNK, gcols), 1) // hd
    bias = sgub_ref[layer]
    wgs = [jnp.concatenate([jnp.where(causal, sguw_ref[g * HEADS_PER_DOT + hh], 0.0).astype(BF16)
                            for hh in range(HEADS_PER_DOT)], axis=1)
           for g in range(SGU_HEADS // HEADS_PER_DOT)]
    mixed_rows = []
    for c in range(tm // CHUNK):
        vc = vb[c * CHUNK:(c + 1) * CHUNK, :]
        outs = []
        for g, wg in enumerate(wgs):
            vg = vc[:, g * gcols:(g + 1) * gcols]
            rhs = jnp.concatenate(
                [jnp.where(lane_head == hh, vg, jnp.zeros((), BF16)) for hh in range(HEADS_PER_DOT)], axis=0)
            outs.append(_dot(wg, rhs))
        mixed_rows.append(jnp.concatenate(outs, axis=1) + bias)
    mixed = jnp.concatenate(mixed_rows, axis=0)
    y_b = u * mixed * _silu(pb[:, 2 * w:3 * w])

    z = pa[:, w:2 * w] * pa[:, 2 * w:3 * w]
    zbuf[zoff:zoff + tm, :] = z
    zfull = zbuf[...]
    conv = p["conv_b"] + (_shift_rows(zfull, 2)[zoff:, :] * p["conv_w0"]
                          + _shift_rows(zfull, 1)[zoff:, :] * p["conv_w1"]
                          + z * p["conv_w2"])
    y_a = pa[:, 0:w] * conv * _silu(pa[:, 3 * w:4 * w])
    zbuf[0:zoff, :] = zbuf[tm:tm + zoff, :]

    convo_ref[0] = z[tm - (CONV_W - 1):tm, :]
    for j in range(POOL_BUF):
        r = coff + tm - POOL_BUF + j
        poolo_ref[j, pl.ds(b, 1), :] = cbuf[r:r + 1, :]
    vo_ref[0] = v[tm - CHUNK:tm, :]
    cbuf[0:coff, :] = cbuf[tm:tm + coff, :]

    da = _dot(y_a.astype(BF16), wbr_ref[0:w, :])
    db = _dot(y_b.astype(BF16), wbr_ref[w:2 * w, :])
    dc = _dot(y_c.astype(BF16), wbr_ref[2 * w:3 * w, :])
    merged = _sigmoid(pg[:, 0:d]) * da + _sigmoid(pg[:, d:2 * d]) * db + _sigmoid(pg[:, 2 * d:3 * d]) * dc
    x_new = x + gate * _dot(merged.astype(BF16), wout_ref[...])
    if final:
        ms = jnp.mean(x_new * x_new, axis=-1, keepdims=True)
        x_new = x_new * lax.rsqrt(ms + EPS) * fg_ref[...]
    xo_ref[0] = x_new


def _decode_rows(xs_ref, mods_ref, sconv_ref, spool_hbm, prm, sguw_ref, win_ref, poolw_ref, wbr_ref, wout_ref, fg_ref,
                 xso_ref, convso_ref, vso_ref, poolso_hbm, state, cx_stage, sem, *, layer, final):
    d = xs_ref.shape[-1]
    w = d // 2
    p = _layer_rows(prm, layer)
    gd = w // len(POOL_WINDOWS)

    load = pltpu.make_async_copy(spool_hbm.at[layer], state, sem.at[0])
    load.start()

    x = xs_ref[...] if len(xs_ref.shape) == 2 else xs_ref[:, 0, :]
    mod = mods_ref[...]
    shift, scale, gate = mod[:, 0:d], mod[:, d:2 * d], mod[:, 2 * d:3 * d]
    hb = _modulated_norm(x, p["norm_g"], shift, scale).astype(BF16)

    pa = _dot(hb, win_ref[:, 0:4 * w])
    z = pa[:, w:2 * w] * pa[:, 2 * w:3 * w]
    prev0, prev1 = sconv_ref[:, 0, :], sconv_ref[:, 1, :]
    conv = p["conv_b"] + (prev0 * p["conv_w0"] + prev1 * p["conv_w1"] + z * p["conv_w2"])
    y_a = pa[:, 0:w] * conv * _silu(pa[:, 3 * w:4 * w])
    convso_ref[:, 0, :] = prev1
    convso_ref[:, 1, :] = z

    pb = _dot(hb, win_ref[:, 4 * w:7 * w])
    u = _gelu(pb[:, 0:w])
    v = _layernorm(_gelu(pb[:, w:2 * w]), p["lnv_g"], p["lnv_b"])
    vso_ref[:, 0, :] = v
    sgu_w0 = _spread_heads([sguw_ref[h, 0:1, 0:1] for h in range(SGU_HEADS)], w)
    sgu_b0 = _spread_heads([prm[-1][layer, h:h + 1, 0:1] for h in range(SGU_HEADS)], w)
    y_b = u * (v * sgu_w0 + sgu_b0) * _silu(pb[:, 2 * w:3 * w])

    pc = _dot(hb, win_ref[:, 7 * w:9 * w])
    c_x = pc[:, 0:w]
    cx_stage[...] = c_x
    newest = pltpu.make_async_copy(cx_stage, poolso_hbm.at[layer, POOL_BUF - 1], sem.at[1])
    newest.start()
    load.wait()
    keep = pltpu.make_async_copy(state.at[pl.ds(1, POOL_BUF - 1)], poolso_hbm.at[layer, pl.ds(0, POOL_BUF - 1)],
                                 sem.at[2])
    keep.start()
    yc_groups = []
    for g, win in enumerate(POOL_WINDOWS):
        s = c_x[:, g * gd:(g + 1) * gd]
        for j in range(POOL_BUF - (win - 1), POOL_BUF):
            s = s + state[j, :, g * gd:(g + 1) * gd]
        pm = s / float(min(win, PAST_LEN + 1)) - c_x[:, g * gd:(g + 1) * gd]
        yc_groups.append(_dot(pm.astype(BF16), poolw_ref[g].astype(BF16)))
    y_c = jnp.concatenate(yc_groups, axis=1)
    y_c = (y_c + p["pool_b"]) * p["pool_scale"] * _silu(pc[:, w:2 * w])

    pg = _dot(hb, win_ref[:, 9 * w:9 * w + 3 * d])
    merged = (_sigmoid(pg[:, 0:d]) * _dot(y_a.astype(BF16), wbr_ref[0:w, :])
              + _sigmoid(pg[:, d:2 * d]) * _dot(y_b.astype(BF16), wbr_ref[w:2 * w, :])
              + _sigmoid(pg[:, 2 * d:3 * d]) * _dot(y_c.astype(BF16), wbr_ref[2 * w:3 * w, :]))
    x_new = x + gate * _dot(merged.astype(BF16), wout_ref[...])
    if final:
        ms = jnp.mean(x_new * x_new, axis=-1, keepdims=True)
        x_new = x_new * lax.rsqrt(ms + EPS) * fg_ref[...]
    if len(xso_ref.shape) == 2:
        xso_ref[...] = x_new
    else:
        xso_ref[:, 0, :] = x_new
    newest.wait()
    keep.wait()


def _layer_kernel(x_ref, mod_ref, win_ref, sguw_ref, sgub_ref, poolw_ref, wbr_ref, wout_ref, fg_ref,
                  xs_ref, mods_ref, sconv_ref, spool_hbm,
                  *rest, layer, final, n_tiles, prompt_steps):
    prm, rest = rest[:N_ROW_PARAMS], rest[N_ROW_PARAMS:]
    rest = rest[N_STATE_OUTPUTS:]
    if final:
        outs, scratch = rest[:8], rest[8:]
    else:
        casts_in, outs, casts_out, scratch = rest[:3], rest[3:11], rest[11:14], rest[14:]
    xo_ref, convo_ref, poolo_ref, vo_ref, xso_ref, convso_ref, vso_ref, poolso_hbm = outs
    zbuf, cbuf, state, cx_stage, sem, shiftw = scratch
    step = pl.program_id(0)

    @pl.when(step < prompt_steps)
    def _():
        if not final:
            for src, dst in zip(casts_in, casts_out):
                dst[...] = src[...].astype(BF16)
        _prompt_tile(step // n_tiles, step % n_tiles, x_ref, mod_ref, prm, win_ref, sguw_ref, sgub_ref,
                     poolw_ref, wbr_ref, wout_ref, fg_ref, xo_ref, convo_ref, poolo_ref, vo_ref, zbuf, cbuf, shiftw,
                     layer=layer, final=final)

    @pl.when(step == prompt_steps)
    def _():
        _decode_rows(xs_ref, mods_ref, sconv_ref, spool_hbm, prm, sguw_ref, win_ref, poolw_ref, wbr_ref, wout_ref,
                     fg_ref,
                     xso_ref, convso_ref, vso_ref, poolso_hbm, state, cx_stage, sem, layer=layer, final=final)


def _layer_call(layer, final, x, xs, mod_p, mod_s, sconv, spool, rows_p, sgub, fg, weights, sguw, poolw, next_f32,
                states):
    n, seq, d = x.shape
    rows = xs.shape[0]
    w = d // 2
    tm = SEQ_TILE
    n_tiles = seq // tm
    prompt_steps = n * n_tiles
    xs_out = jax.ShapeDtypeStruct((rows, 1, d) if final else (rows, d), F32)
    resident = dict(pipeline_mode=pl.Buffered(1))
    whole = lambda a: pl.BlockSpec(a.shape, lambda s: (0,) * a.ndim, **resident)
    of_layer = lambda a: pl.BlockSpec((None,) + a.shape[1:], lambda s: (layer,) + (0,) * (a.ndim - 1), **resident)
    win, wbr, wout = weights

    ptile = lambda s: jnp.minimum(s, prompt_steps - 1)
    x_tile = lambda s: (ptile(s) // n_tiles, ptile(s) % n_tiles, 0)
    batch_row = lambda s: (layer, ptile(s) // n_tiles, 0, 0)
    hbm = pl.BlockSpec(memory_space=pl.ANY)

    in_specs = [
        pl.BlockSpec((1, tm, d), x_tile),
        whole(mod_p), whole(win), of_layer(sguw), whole(sgub), of_layer(poolw), whole(wbr), whole(wout), whole(fg),
        whole(xs), of_layer(mod_s), of_layer(sconv), hbm,
        *[whole(a) for a in rows_p],
    ]
    out_specs = [
        pl.BlockSpec((1, tm, d), x_tile),
        pl.BlockSpec((None, 1, CONV_W - 1, w), batch_row),
        pl.BlockSpec((None, POOL_BUF, n, w), lambda s: (layer, 0, 0, 0)),
        pl.BlockSpec((None, 1, CHUNK, w), batch_row),
        pl.BlockSpec(xs_out.shape, lambda s: (0,) * len(xs_out.shape)),
        pl.BlockSpec((None, rows, CONV_W - 1, w), lambda s: (layer, 0, 0, 0)),
        pl.BlockSpec((None, rows, 1, w), lambda s: (layer, 0, 0, 0)),
        hbm,
    ]
    conv_p, pool_p, v_p, conv_s, v_s, pool_s = (jax.ShapeDtypeStruct(a.shape, F32) for a in states)
    out_shape = [jax.ShapeDtypeStruct((n, seq, d), F32), conv_p, pool_p, v_p,
                 xs_out, conv_s, v_s, pool_s]
    state_outputs = (1, 2, 3, 5, 6, 7)
    assert len(state_outputs) == len(states) == N_STATE_OUTPUTS
    assert len(rows_p) == N_ROW_PARAMS
    args = [x, mod_p, win, sguw, sgub, poolw, wbr, wout, fg, xs, mod_s, sconv, spool, *rows_p]
    aliases = {len(args) + i: o for i, o in enumerate(state_outputs)}
    in_specs += [hbm] * len(states)
    args += list(states)
    if not final:
        for src in next_f32:
            nrows, cols = src.shape[1:]
            assert nrows % prompt_steps == 0 and (nrows // prompt_steps) % BF16_SUBLANES == 0
            blk = nrows // prompt_steps
            in_specs.append(pl.BlockSpec((None, blk, cols), lambda s: (layer + 1, ptile(s), 0)))
            out_specs.append(pl.BlockSpec((blk, cols), lambda s: (ptile(s), 0)))
            out_shape.append(jax.ShapeDtypeStruct((nrows, cols), BF16))
            args.append(src)
    scratch_shapes = [
        pltpu.VMEM((SUBLANES + tm, w), F32),
        pltpu.VMEM((2 * SUBLANES + tm, w), F32),
        pltpu.VMEM((POOL_BUF, rows, w), F32),
        pltpu.VMEM((rows, w), F32),
        pltpu.SemaphoreType.DMA((3,)),
        pltpu.VMEM((n, win.shape[1]), F32),
    ]
    return pl.pallas_call(
        functools.partial(_layer_kernel, layer=layer, final=final, n_tiles=n_tiles, prompt_steps=prompt_steps),
        grid=(prompt_steps + 1,),
        in_specs=in_specs,
        out_specs=out_specs,
        out_shape=out_shape,
        scratch_shapes=scratch_shapes,
        input_output_aliases=aliases,
        compiler_params=pltpu.CompilerParams(
            dimension_semantics=("arbitrary",),
            vmem_limit_bytes=VMEM_LIMIT_BYTES),
        name=f"layer{layer}",
    )(*args)


def kernel(x_prompt, x_sample, c_prompt, c_sample, state_conv, state_pool, w_ada, b_ada, norm_g, w_in, conv_w,
           conv_b, lnv_g, lnv_b, sgu_w, sgu_b, pool_w, pool_b, pool_scale, w_branch, w_out, final_g):
    n, seq, d = x_prompt.shape
    rows = x_sample.shape[0]
    depth = w_in.shape[0]
    w = d // 2
    assert x_sample.shape[1] == 1 and seq % SEQ_TILE == 0 and SEQ_TILE % CHUNK == 0

    next_f32 = (w_in, w_branch.reshape(depth, N_BRANCH * w, d), w_out)
    state_shapes = ((depth, n, CONV_W - 1, w), (depth, POOL_BUF, n, w), (depth, n, CHUNK, w),
                    (depth, rows, CONV_W - 1, w), (depth, rows, 1, w), (depth, POOL_BUF, rows, w))
    mod_p, mod_s, sgub_full, *made = _mod_call(c_prompt, c_sample, w_ada, b_ada, sgu_b, next_f32, state_shapes)
    weights, states = made[:len(next_f32)], tuple(made[len(next_f32):])

    rows_p = (norm_g, conv_b, lnv_g, lnv_b, pool_b, pool_scale, jnp.transpose(conv_w, (1, 0, 2)), sgu_b)
    fg = final_g.reshape(1, d)
    hist_major = lambda a: jnp.transpose(a, (0, 2, 1, 3))
    spool = hist_major(state_pool)

    xp, xs = x_prompt, x_sample
    for l in range(depth):
        xp, conv_p, pool_p, v_p, xs, conv_s, v_s, pool_s, *next_weights = _layer_call(
            l, l == depth - 1, xp, xs, mod_p, mod_s, state_conv, spool, rows_p, sgub_full, fg, tuple(weights), sgu_w,
            pool_w, next_f32, states)
        states = (conv_p, pool_p, v_p, conv_s, v_s, pool_s)
        weights = next_weights

    return xp, xs, conv_p, conv_s, hist_major(pool_p), hist_major(pool_s), v_p, v_s
```

```python
import functools
import math

import jax
import jax.numpy as jnp
from jax import lax
from jax.experimental import pallas as pl
from jax.experimental.pallas import tpu as pltpu

F32 = jnp.float32
BF16 = jnp.bfloat16

CONV_W = 3
CHUNK = 128
SGU_HEADS = 8
POOL_WINDOWS = (2, 4, 8, 16)
POOL_BUF = max(POOL_WINDOWS) - 1
PAST_LEN = 16384
EPS = 1e-6

N_BRANCH = 3
N_ROW_PARAMS = 8
N_STATE_OUTPUTS = 6
SUBLANES = 8
BF16_SUBLANES = 16
SEQ_TILE = 512
HEADS_PER_DOT = 2
VMEM_LIMIT_BYTES = 58 * 1024 * 1024


def _sigmoid(x):
    return 0.5 * (1.0 + jnp.tanh(0.5 * x))


def _silu(x):
    return x * _sigmoid(x)


def _gelu(x):
    c = math.sqrt(2.0 / math.pi)
    return 0.5 * x * (1.0 + jnp.tanh(c * (x + 0.044715 * (x * x * x))))


def _shift_rows(x, k):
    return pltpu.roll(x, k, axis=0)


def _dot(a, b):
    return jnp.dot(a, b, preferred_element_type=F32)


def _modulated_norm(x, norm_g, shift, scale):
    ms = jnp.mean(x * x, axis=-1, keepdims=True)
    return (x * lax.rsqrt(ms + EPS)) * (norm_g * (1.0 + scale)) + shift


def _layer_rows(prm, layer):
    ng, convb, lng, lnb, poolb, pools, convw, _ = prm
    row = lambda ref: ref[layer:layer + 1, :]
    out = dict(norm_g=row(ng), conv_b=row(convb), lnv_g=row(lng), lnv_b=row(lnb), pool_b=row(poolb),
               pool_scale=row(pools))
    for k in range(CONV_W):
        out[f"conv_w{k}"] = convw[k, layer:layer + 1, :]
    return out


def _spread_heads(per_head, w):
    hd = w // len(per_head)
    rows = per_head[0].shape[0]
    lane_head = lax.broadcasted_iota(jnp.int32, (rows, w), 1) // hd
    out = jnp.zeros((rows, w), F32)
    for h, val in enumerate(per_head):
        out = jnp.where(lane_head == h, val, out)
    return out


def _layernorm(x, g, b):
    mu = jnp.mean(x, axis=-1, keepdims=True)
    d = x - mu
    var = jnp.mean(d * d, axis=-1, keepdims=True)
    return d * lax.rsqrt(var + EPS) * g + b


def _mod_kernel(cp_ref, cs_ref, w_ref, b_ref, sgub_ref, *rest, n_cast):
    casts_in, (op_ref, os_ref, bias_ref), rest = rest[:n_cast], rest[n_cast:n_cast + 3], rest[n_cast + 3:]
    casts_out, state_refs = rest[:n_cast], rest[n_cast:]
    by_position = sgub_ref[pl.program_id(0)].T
    bias_ref[0] = _spread_heads([by_position[:, h:h + 1] for h in range(SGU_HEADS)], bias_ref.shape[-1])
    for ref in state_refs:
        ref[...] = jnp.zeros(ref.shape, ref.dtype)
    n = cp_ref.shape[0]
    c = jnp.concatenate([cp_ref[...], cp_ref[...], cs_ref[...]], axis=0)
    m = _dot(_silu(c).astype(BF16), w_ref[0].astype(BF16)) + b_ref[pl.ds(pl.program_id(0), 1), :]
    op_ref[0] = m[0:n]
    os_ref[0] = m[2 * n:]
    for src, dst in zip(casts_in, casts_out):
        dst[...] = src[...].astype(BF16)


def _mod_call(c_prompt, c_sample, w_ada, b_ada, sgu_b, first_f32, state_shapes):
    depth, d, d3 = w_ada.shape
    chunk = sgu_b.shape[2]
    n, rows = c_prompt.shape[0], c_sample.shape[0]
    assert n == SUBLANES
    n_col = 4
    col_block = d3 // n_col
    steps = depth * n_col
    in_specs = [
        pl.BlockSpec((n, d), lambda l, j: (0, 0)),
        pl.BlockSpec((rows, d), lambda l, j: (0, 0)),
        pl.BlockSpec((1, d, col_block), lambda l, j: (l, 0, j)),
        pl.BlockSpec((depth, col_block), lambda l, j: (0, j)),
        pl.BlockSpec(sgu_b.shape, lambda l, j: (0, 0, 0)),
    ]
    out_specs = [
        pl.BlockSpec((1, n, col_block), lambda l, j: (l, 0, j)),
        pl.BlockSpec((1, rows, col_block), lambda l, j: (l, 0, j)),
        pl.BlockSpec((1, chunk, d // 2), lambda l, j: (l, 0, 0)),
    ]
    out_shape = [
        jax.ShapeDtypeStruct((depth, n, d3), F32),
        jax.ShapeDtypeStruct((depth, rows, d3), F32),
        jax.ShapeDtypeStruct((depth, chunk, d // 2), F32),
    ]
    for src in first_f32:
        nrows, cols = src.shape[1:]
        assert nrows % steps == 0 and (nrows // steps) % BF16_SUBLANES == 0
        blk = nrows // steps
        in_specs.append(pl.BlockSpec((None, blk, cols), lambda l, j: (0, l * n_col + j, 0)))
        out_specs.append(pl.BlockSpec((blk, cols), lambda l, j: (l * n_col + j, 0)))
        out_shape.append(jax.ShapeDtypeStruct((nrows, cols), BF16))
    for shape in state_shapes:
        assert shape[0] == depth and len(shape) == 4
        if shape[1] % n_col == 0:
            axis = 1
        elif (shape[2] // n_col) % SUBLANES == 0 and shape[2] % n_col == 0:
            axis = 2
        else:
            axis = None
        block = tuple(1 if a == 0 else (s // n_col if a == axis else s) for a, s in enumerate(shape))
        out_specs.append(pl.BlockSpec(block, lambda l, j, axis=axis: tuple(
            l if a == 0 else (j if a == axis else 0) for a in range(4))))
        out_shape.append(jax.ShapeDtypeStruct(shape, F32))
    return pl.pallas_call(
        functools.partial(_mod_kernel, n_cast=len(first_f32)),
        grid=(depth, n_col),
        in_specs=in_specs,
        out_specs=out_specs,
        out_shape=out_shape,
        compiler_params=pltpu.CompilerParams(
            dimension_semantics=("arbitrary", "arbitrary"),
            vmem_limit_bytes=VMEM_LIMIT_BYTES),
        name="adaln_mod",
    )(c_prompt, c_sample, w_ada, b_ada, sgu_b, *first_f32)


def _prompt_tile(b, t, x_ref, mod_ref, prm, win_ref, sguw_ref, sgub_ref, poolw_ref, wbr_ref, wout_ref, fg_ref,
                 xo_ref, convo_ref, poolo_ref, vo_ref, zbuf, cbuf, *, layer, final):
    tm, d = x_ref.shape[1], x_ref.shape[2]
    w = d // 2
    p = _layer_rows(prm, layer)
    zoff = SUBLANES
    coff = 2 * SUBLANES

    @pl.when(t == 0)
    def _():
        zbuf[0:zoff, :] = jnp.zeros((zoff, w), F32)
        cbuf[0:coff, :] = jnp.zeros((coff, w), F32)

    x = x_ref[0]
    mod = mod_ref[layer, pl.ds(b, 1), :]
    shift, scale, gate = mod[:, 0:d], mod[:, d:2 * d], mod[:, 2 * d:3 * d]
    hb = _modulated_norm(x, p["norm_g"], shift, scale).astype(BF16)

    pc = _dot(hb, win_ref[:, 7 * w:9 * w])
    pb = _dot(hb, win_ref[:, 4 * w:7 * w])
    pa = _dot(hb, win_ref[:, 0:4 * w])
    pg = _dot(hb, win_ref[:, 9 * w:9 * w + 3 * d])

    c_x = pc[:, 0:w]
    cbuf[coff:coff + tm, :] = c_x
    gd = w // len(POOL_WINDOWS)
    pos1 = lax.broadcasted_iota(jnp.int32, (tm, gd), 0) + (t * tm + 1)
    yc_groups = []
    for g, win in enumerate(POOL_WINDOWS):
        cols = slice(g * gd, (g + 1) * gd)
        s = cbuf[:, cols]
        k = 1
        while k < win:
            s = s + _shift_rows(s, k)
            k *= 2
        s = s[coff:, :]
        cnt = jnp.minimum(pos1, win).astype(F32)
        pm = s / cnt - c_x[:, cols]
        yc_groups.append(_dot(pm.astype(BF16), poolw_ref[g].astype(BF16)))
    y_c = jnp.concatenate(yc_groups, axis=1)
    y_c = (y_c + p["pool_b"]) * p["pool_scale"] * _silu(pc[:, w:2 * w])

    u = _gelu(pb[:, 0:w])
    v = _layernorm(_gelu(pb[:, w:2 * w]), p["lnv_g"], p["lnv_b"])
    vb = v.astype(BF16)
    causal = (lax.broadcasted_iota(jnp.int32, (CHUNK, CHUNK), 1)
              <= lax.broadcasted_iota(jnp.int32, (CHUNK, CHUNK), 0))
    hd = w // SGU_HEADS
    gcols = HEADS_PER_DOT * hd
    lane_head = lax.broadcasted_iota(jnp.int32, (CHUNK, gcols), 1) // hd
    bias = sgub_ref[layer]
    wgs = [jnp.concatenate([jnp.where(causal, sguw_ref[g * HEADS_PER_DOT + hh], 0.0).astype(BF16)
                            for hh in range(HEADS_PER_DOT)], axis=1)
           for g in range(SGU_HEADS // HEADS_PER_DOT)]
    mixed_rows = []
    for c in range(tm // CHUNK):
        vc = vb[c * CHUNK:(c + 1) * CHUNK, :]
        outs = []
        for g, wg in enumerate(wgs):
            vg = vc[:, g * gcols:(g + 1) * gcols]
            rhs = jnp.concatenate(
                [jnp.where(lane_head == hh, vg, jnp.zeros((), BF16)) for hh in range(HEADS_PER_DOT)], axis=0)
            outs.append(_dot(wg, rhs))
        mixed_rows.append(jnp.concatenate(outs, axis=1) + bias)
    mixed = jnp.concatenate(mixed_rows, axis=0)
    y_b = u * mixed * _silu(pb[:, 2 * w:3 * w])

    z = pa[:, w:2 * w] * pa[:, 2 * w:3 * w]
    zbuf[zoff:zoff + tm, :] = z
    zfull = zbuf[...]
    conv = p["conv_b"] + (_shift_rows(zfull, 2)[zoff:, :] * p["conv_w0"]
                          + _shift_rows(zfull, 1)[zoff:, :] * p["conv_w1"]
                          + z * p["conv_w2"])
    y_a = pa[:, 0:w] * conv * _silu(pa[:, 3 * w:4 * w])
    zbuf[0:zoff, :] = zbuf[tm:tm + zoff, :]

    convo_ref[0] = z[tm - (CONV_W - 1):tm, :]
    for j in range(POOL_BUF):
        r = coff + tm - POOL_BUF + j
        poolo_ref[j, pl.ds(b, 1), :] = cbuf[r:r + 1, :]
    vo_ref[0] = v[tm - CHUNK:tm, :]
    cbuf[0:coff, :] = cbuf[tm:tm + coff, :]

    da = _dot(y_a.astype(BF16), wbr_ref[0:w, :])
    db = _dot(y_b.astype(BF16), wbr_ref[w:2 * w, :])
    dc = _dot(y_c.astype(BF16), wbr_ref[2 * w:3 * w, :])
    merged = _sigmoid(pg[:, 0:d]) * da + _sigmoid(pg[:, d:2 * d]) * db + _sigmoid(pg[:, 2 * d:3 * d]) * dc
    x_new = x + gate * _dot(merged.astype(BF16), wout_ref[...])
    if final:
        ms = jnp.mean(x_new * x_new, axis=-1, keepdims=True)
        x_new = x_new * lax.rsqrt(ms + EPS) * fg_ref[...]
    xo_ref[0] = x_new


def _decode_rows(xs_ref, mods_ref, sconv_ref, spool_hbm, prm, sguw_ref, win_ref, poolw_ref, wbr_ref, wout_ref, fg_ref,
                 xso_ref, convso_ref, vso_ref, poolso_hbm, state, cx_stage, sem, *, layer, final):
    d = xs_ref.shape[-1]
    w = d // 2
    p = _layer_rows(prm, layer)
    gd = w // len(POOL_WINDOWS)

    load = pltpu.make_async_copy(spool_hbm.at[layer], state, sem.at[0])
    load.start()

    x = xs_ref[...] if len(xs_ref.shape) == 2 else xs_ref[:, 0, :]
    mod = mods_ref[...]
    shift, scale, gate = mod[:, 0:d], mod[:, d:2 * d], mod[:, 2 * d:3 * d]
    hb = _modulated_norm(x, p["norm_g"], shift, scale).astype(BF16)

    pa = _dot(hb, win_ref[:, 0:4 * w])
    z = pa[:, w:2 * w] * pa[:, 2 * w:3 * w]
    prev0, prev1 = sconv_ref[:, 0, :], sconv_ref[:, 1, :]
    conv = p["conv_b"] + (prev0 * p["conv_w0"] + prev1 * p["conv_w1"] + z * p["conv_w2"])
    y_a = pa[:, 0:w] * conv * _silu(pa[:, 3 * w:4 * w])
    convso_ref[:, 0, :] = prev1
    convso_ref[:, 1, :] = z

    pb = _dot(hb, win_ref[:, 4 * w:7 * w])
    u = _gelu(pb[:, 0:w])
    v = _layernorm(_gelu(pb[:, w:2 * w]), p["lnv_g"], p["lnv_b"])
    vso_ref[:, 0, :] = v
    sgu_w0 = _spread_heads([sguw_ref[h, 0:1, 0:1] for h in range(SGU_HEADS)], w)
    sgu_b0 = _spread_heads([prm[-1][layer, h:h + 1, 0:1] for h in range(SGU_HEADS)], w)
    y_b = u * (v * sgu_w0 + sgu_b0) * _silu(pb[:, 2 * w:3 * w])

    pc = _dot(hb, win_ref[:, 7 * w:9 * w])
    c_x = pc[:, 0:w]
    cx_stage[...] = c_x
    newest = pltpu.make_async_copy(cx_stage, poolso_hbm.at[layer, POOL_BUF - 1], sem.at[1])
    newest.start()
    load.wait()
    keep = pltpu.make_async_copy(state.at[pl.ds(1, POOL_BUF - 1)], poolso_hbm.at[layer, pl.ds(0, POOL_BUF - 1)],
                                 sem.at[2])
    keep.start()
    yc_groups = []
    for g, win in enumerate(POOL_WINDOWS):
        s = c_x[:, g * gd:(g + 1) * gd]
        for j in range(POOL_BUF - (win - 1), POOL_BUF):
            s = s + state[j, :, g * gd:(g + 1) * gd]
        pm = s / float(min(win, PAST_LEN + 1)) - c_x[:, g * gd:(g + 1) * gd]
        yc_groups.append(_dot(pm.astype(BF16), poolw_ref[g].astype(BF16)))
    y_c = jnp.concatenate(yc_groups, axis=1)
    y_c = (y_c + p["pool_b"]) * p["pool_scale"] * _silu(pc[:, w:2 * w])

    pg = _dot(hb, win_ref[:, 9 * w:9 * w + 3 * d])
    merged = (_sigmoid(pg[:, 0:d]) * _dot(y_a.astype(BF16), wbr_ref[0:w, :])
              + _sigmoid(pg[:, d:2 * d]) * _dot(y_b.astype(BF16), wbr_ref[w:2 * w, :])
              + _sigmoid(pg[:, 2 * d:3 * d]) * _dot(y_c.astype(BF16), wbr_ref[2 * w:3 * w, :]))
    x_new = x + gate * _dot(merged.astype(BF16), wout_ref[...])
    if final:
        ms = jnp.mean(x_new * x_new, axis=-1, keepdims=True)
        x_new = x_new * lax.rsqrt(ms + EPS) * fg_ref[...]
    if len(xso_ref.shape) == 2:
        xso_ref[...] = x_new
    else:
        xso_ref[:, 0, :] = x_new
    newest.wait()
    keep.wait()


def _layer_kernel(x_ref, mod_ref, win_ref, sguw_ref, sgub_ref, poolw_ref, wbr_ref, wout_ref, fg_ref,
                  xs_ref, mods_ref, sconv_ref, spool_hbm,
                  *rest, layer, final, n_tiles, prompt_steps):
    prm, rest = rest[:N_ROW_PARAMS], rest[N_ROW_PARAMS:]
    rest = rest[N_STATE_OUTPUTS:]
    if final:
        outs, scratch = rest[:8], rest[8:]
    else:
        casts_in, outs, casts_out, scratch = rest[:3], rest[3:11], rest[11:14], rest[14:]
    xo_ref, convo_ref, poolo_ref, vo_ref, xso_ref, convso_ref, vso_ref, poolso_hbm = outs
    zbuf, cbuf, state, cx_stage, sem = scratch
    step = pl.program_id(0)

    tile = step - 1

    @pl.when(step > 0)
    def _():
        if not final:
            for src, dst in zip(casts_in, casts_out):
                dst[...] = src[...].astype(BF16)
        _prompt_tile(tile // n_tiles, tile % n_tiles, x_ref, mod_ref, prm, win_ref, sguw_ref, sgub_ref,
                     poolw_ref, wbr_ref, wout_ref, fg_ref, xo_ref, convo_ref, poolo_ref, vo_ref, zbuf, cbuf,
                     layer=layer, final=final)

    @pl.when(step == 0)
    def _():
        _decode_rows(xs_ref, mods_ref, sconv_ref, spool_hbm, prm, sguw_ref, win_ref, poolw_ref, wbr_ref, wout_ref,
                     fg_ref,
                     xso_ref, convso_ref, vso_ref, poolso_hbm, state, cx_stage, sem, layer=layer, final=final)


def _layer_call(layer, final, x, xs, mod_p, mod_s, sconv, spool, rows_p, sgub, fg, weights, sguw, poolw, next_f32,
                states):
    n, seq, d = x.shape
    rows = xs.shape[0]
    w = d // 2
    tm = SEQ_TILE
    n_tiles = seq // tm
    prompt_steps = n * n_tiles
    xs_out = jax.ShapeDtypeStruct((rows, 1, d) if final else (rows, d), F32)
    resident = dict(pipeline_mode=pl.Buffered(1))
    whole = lambda a: pl.BlockSpec(a.shape, lambda s: (0,) * a.ndim, **resident)
    of_layer = lambda a: pl.BlockSpec((None,) + a.shape[1:], lambda s: (layer,) + (0,) * (a.ndim - 1), **resident)
    win, wbr, wout = weights

    ptile = lambda s: jnp.maximum(s - 1, 0)
    x_tile = lambda s: (ptile(s) // n_tiles, ptile(s) % n_tiles, 0)
    batch_row = lambda s: (layer, ptile(s) // n_tiles, 0, 0)
    hbm = pl.BlockSpec(memory_space=pl.ANY)

    in_specs = [
        pl.BlockSpec((1, tm, d), x_tile),
        whole(mod_p), whole(win), of_layer(sguw), whole(sgub), of_layer(poolw), whole(wbr), whole(wout), whole(fg),
        whole(xs), of_layer(mod_s), of_layer(sconv), hbm,
        *[whole(a) for a in rows_p],
    ]
    out_specs = [
        pl.BlockSpec((1, tm, d), x_tile),
        pl.BlockSpec((None, 1, CONV_W - 1, w), batch_row),
        pl.BlockSpec((None, POOL_BUF, n, w), lambda s: (layer, 0, 0, 0)),
        pl.BlockSpec((None, 1, CHUNK, w), batch_row),
        pl.BlockSpec(xs_out.shape, lambda s: (0,) * len(xs_out.shape)),
        pl.BlockSpec((None, rows, CONV_W - 1, w), lambda s: (layer, 0, 0, 0)),
        pl.BlockSpec((None, rows, 1, w), lambda s: (layer, 0, 0, 0)),
        hbm,
    ]
    conv_p, pool_p, v_p, conv_s, v_s, pool_s = (jax.ShapeDtypeStruct(a.shape, F32) for a in states)
    out_shape = [jax.ShapeDtypeStruct((n, seq, d), F32), conv_p, pool_p, v_p,
                 xs_out, conv_s, v_s, pool_s]
    state_outputs = (1, 2, 3, 5, 6, 7)
    assert len(state_outputs) == len(states) == N_STATE_OUTPUTS
    assert len(rows_p) == N_ROW_PARAMS
    args = [x, mod_p, win, sguw, sgub, poolw, wbr, wout, fg, xs, mod_s, sconv, spool, *rows_p]
    aliases = {len(args) + i: o for i, o in enumerate(state_outputs)}
    in_specs += [hbm] * len(states)
    args += list(states)
    if not final:
        for src in next_f32:
            nrows, cols = src.shape[1:]
            assert nrows % prompt_steps == 0 and (nrows // prompt_steps) % BF16_SUBLANES == 0
            blk = nrows // prompt_steps
            in_specs.append(pl.BlockSpec((None, blk, cols), lambda s: (layer + 1, ptile(s), 0)))
            out_specs.append(pl.BlockSpec((blk, cols), lambda s: (ptile(s), 0)))
            out_shape.append(jax.ShapeDtypeStruct((nrows, cols), BF16))
            args.append(src)
    scratch_shapes = [
        pltpu.VMEM((SUBLANES + tm, w), F32),
        pltpu.VMEM((2 * SUBLANES + tm, w), F32),
        pltpu.VMEM((POOL_BUF, rows, w), F32),
        pltpu.VMEM((rows, w), F32),
        pltpu.SemaphoreType.DMA((3,)),
    ]
    return pl.pallas_call(
        functools.partial(_layer_kernel, layer=layer, final=final, n_tiles=n_tiles, prompt_steps=prompt_steps),
        grid=(prompt_steps + 1,),
        in_specs=in_specs,
        out_specs=out_specs,
        out_shape=out_shape,
        scratch_shapes=scratch_shapes,
        input_output_aliases=aliases,
        compiler_params=pltpu.CompilerParams(
            dimension_semantics=("arbitrary",),
            vmem_limit_bytes=VMEM_LIMIT_BYTES),
        name=f"layer{layer}",
    )(*args)


def kernel(x_prompt, x_sample, c_prompt, c_sample, state_conv, state_pool, w_ada, b_ada, norm_g, w_in, conv_w,
           conv_b, lnv_g, lnv_b, sgu_w, sgu_b, pool_w, pool_b, pool_scale, w_branch, w_out, final_g):
    n, seq, d = x_prompt.shape
    rows = x_sample.shape[0]
    depth = w_in.shape[0]
    w = d // 2
    assert x_sample.shape[1] == 1 and seq % SEQ_TILE == 0 and SEQ_TILE % CHUNK == 0

    next_f32 = (w_in, w_branch.reshape(depth, N_BRANCH * w, d), w_out)
    state_shapes = ((depth, n, CONV_W - 1, w), (depth, POOL_BUF, n, w), (depth, n, CHUNK, w),
                    (depth, rows, CONV_W - 1, w), (depth, rows, 1, w), (depth, POOL_BUF, rows, w))
    mod_p, mod_s, sgub_full, *made = _mod_call(c_prompt, c_sample, w_ada, b_ada, sgu_b, next_f32, state_shapes)
    weights, states = made[:len(next_f32)], tuple(made[len(next_f32):])

    rows_p = (norm_g, conv_b, lnv_g, lnv_b, pool_b, pool_scale, jnp.transpose(conv_w, (1, 0, 2)), sgu_b)
    fg = final_g.reshape(1, d)
    hist_major = lambda a: jnp.transpose(a, (0, 2, 1, 3))
    spool = hist_major(state_pool)

    xp, xs = x_prompt, x_sample
    for l in range(depth):
        xp, conv_p, pool_p, v_p, xs, conv_s, v_s, pool_s, *next_weights = _layer_call(
            l, l == depth - 1, xp, xs, mod_p, mod_s, state_conv, spool, rows_p, sgub_full, fg, tuple(weights), sgu_w,
            pool_w, next_f32, states)
        states = (conv_p, pool_p, v_p, conv_s, v_s, pool_s)
        weights = next_weights

    return xp, xs, conv_p, conv_s, hist_major(pool_p), hist_major(pool_s), v_p, v_s
```
